```python
import jax, jax.numpy as jnp
from jax import lax
import numpy as np

D_MODEL = 1024
BATCH = 2
SEQ = 16384
DEPTH = 2

N_A = DEPTH // 2
N_B = DEPTH - N_A
CHUNK = 64
LEFT_CHUNKS = 8
BAND = (LEFT_CHUNKS + 1) * CHUNK
PAD = LEFT_CHUNKS * CHUNK
D_RNN = D_MODEL
LRU_BLOCKS = 8
LRU_BW = D_RNN // LRU_BLOCKS
CONV_W = 4
LRU_C = 8.0
N_HEADS = 16
HEAD_DIM = D_MODEL // N_HEADS
MAX_REL = 2 * CHUNK
MIN_REL = -(CHUNK - 1)
NREL = MAX_REL - MIN_REL + 1
D_FF = 4 * D_MODEL
EPS = 1e-6

kernel_name = 'yoco_rglru_chunk_relbias_hybrid'


def rmsnorm(x, g):
    xf = x.astype(jnp.float32)
    y = xf * lax.rsqrt(jnp.mean(xf * xf, axis=-1, keepdims=True) + EPS)
    return (y * g.astype(jnp.float32)).astype(x.dtype)


def causal_depthwise_conv(x, w, b):
    s = x.shape[1]
    xp = jnp.pad(x, ((0, 0), (CONV_W - 1, 0), (0, 0)))
    out = b + xp[:, 0:s] * w[0]
    for k in range(1, CONV_W):
        out = out + xp[:, k:k + s] * w[k]
    return out


def rg_lru(x, w_gate, b_gate, lam):
    bsz, s, _ = x.shape
    xf = x.astype(jnp.float32)
    xb = xf.reshape(bsz, s, LRU_BLOCKS, LRU_BW)
    g = jnp.einsum('bsnd,nde->bsne', xb, w_gate.astype(jnp.float32)) + b_gate.astype(jnp.float32)
    r = jax.nn.sigmoid(g[..., :LRU_BW]).reshape(bsz, s, D_RNN)
    i = jax.nn.sigmoid(g[..., LRU_BW:]).reshape(bsz, s, D_RNN)
    log_a = -LRU_C * r * jax.nn.softplus(-lam.astype(jnp.float32))
    a = jnp.exp(log_a)
    b = jnp.sqrt(-jnp.expm1(2.0 * log_a)) * (i * xf)

    def combine(left, right):
        a1, b1 = left
        a2, b2 = right
        return a1 * a2, a2 * b1 + b2

    _, h = lax.associative_scan(combine, (a, b), axis=1)
    return h.astype(x.dtype)


def recurrent_block(x, w_in, conv_w, conv_b, w_gate, b_gate, lam, w_out):
    u = x @ w_in
    gate, rec = u[..., :D_RNN], u[..., D_RNN:]
    rec = causal_depthwise_conv(rec, conv_w, conv_b)
    h = rg_lru(rec, w_gate, b_gate, lam)
    return (jax.nn.gelu(gate) * h) @ w_out


def shared_kv(x, kv_norm, w_kv, k_norm):
    bsz, s, _ = x.shape
    h = rmsnorm(x, kv_norm)
    kv = (h @ w_kv).reshape(bsz, s, 2, N_HEADS, HEAD_DIM)
    k = rmsnorm(kv[:, :, 0], k_norm)
    v = kv[:, :, 1]
    k = jnp.pad(k.transpose(0, 2, 1, 3), ((0, 0), (0, 0), (PAD, 0), (0, 0)))
    v = jnp.pad(v.transpose(0, 2, 1, 3), ((0, 0), (0, 0), (PAD, 0), (0, 0)))
    return k, v


def chunk_band_attention(q, k_pad, v_pad, rel_bias):
    bsz, s = q.shape[:2]
    nc = s // CHUNK
    qc = q.reshape(bsz, nc, CHUNK, N_HEADS, HEAD_DIM).transpose(1, 0, 3, 2, 4)
    qi = jnp.arange(CHUNK)[:, None]
    kj = jnp.arange(BAND)[None, :]
    dist = qi + PAD - kj
    idx = jnp.clip(dist, MIN_REL, MAX_REL) - MIN_REL
    bias = rel_bias.astype(jnp.float32)[:, idx]
    scale = HEAD_DIM ** -0.5

    def one_chunk(args):
        c, qb = args
        kb = lax.dynamic_slice_in_dim(k_pad, c * CHUNK, BAND, axis=2)
        vb = lax.dynamic_slice_in_dim(v_pad, c * CHUNK, BAND, axis=2)
        sc = jnp.einsum('bhqd,bhkd->bhqk', qb, kb).astype(jnp.float32) * scale + bias
        valid = (c * CHUNK - PAD + jnp.arange(BAND)) >= 0
        sc = jnp.where(valid, sc, -jnp.inf)
        p = jax.nn.softmax(sc, axis=-1).astype(vb.dtype)
        return jnp.einsum('bhqk,bhkd->bhqd', p, vb)

    o = lax.map(one_chunk, (jnp.arange(nc), qc))
    return o.transpose(1, 0, 3, 2, 4).reshape(bsz, s, N_HEADS * HEAD_DIM)


def sqrelu_mlp(x, w_up, w_down):
    return jnp.square(jax.nn.relu(x @ w_up)) @ w_down


def setup_inputs(seed: int = 0) -> dict:
    key = jax.random.key(seed)
    ks = jax.random.split(key, 24)
    f32 = jnp.float32

    def nrm(k, shape, fan_in):
        return jax.random.normal(k, shape, f32) * (fan_in ** -0.5)

    def gain(k, shape):
        return 1.0 + 0.05 * jax.random.normal(k, shape, f32)

    x = jax.random.normal(ks[0], (BATCH, SEQ, D_MODEL), f32)
    a_norm = gain(ks[1], (N_A, D_MODEL))
    a_w_in = nrm(ks[2], (N_A, D_MODEL, 2 * D_RNN), D_MODEL)
    a_conv_w = nrm(ks[3], (N_A, CONV_W, D_RNN), CONV_W)
    a_conv_b = 0.01 * jax.random.normal(ks[4], (N_A, D_RNN), f32)
    a_w_gate = nrm(ks[5], (N_A, LRU_BLOCKS, LRU_BW, 2 * LRU_BW), LRU_BW)
    a_b_gate = 0.01 * jax.random.normal(ks[6], (N_A, LRU_BLOCKS, 2 * LRU_BW), f32)
    u = jax.random.uniform(ks[7], (N_A, D_RNN), f32, 0.9, 0.999)
    base = u ** (1.0 / LRU_C)
    a_lambda = jnp.log(base) - jnp.log1p(-base)
    a_w_out = nrm(ks[8], (N_A, D_RNN, D_MODEL), D_RNN)
    kv_norm = gain(ks[9], (D_MODEL,))
    w_kv = nrm(ks[10], (D_MODEL, 2 * N_HEADS * HEAD_DIM), D_MODEL)
    k_norm = gain(ks[11], (HEAD_DIM,))
    b_norm = gain(ks[12], (N_B, D_MODEL))
    b_w_q = nrm(ks[13], (N_B, D_MODEL, N_HEADS * HEAD_DIM), D_MODEL)
    b_q_norm = gain(ks[14], (N_B, HEAD_DIM))
    b_rel_bias = 0.1 * jax.random.normal(ks[15], (N_B, N_HEADS, NREL), f32)
    b_w_o = nrm(ks[16], (N_B, N_HEADS * HEAD_DIM, D_MODEL), N_HEADS * HEAD_DIM)
    mlp_norm = gain(ks[17], (DEPTH, D_MODEL))
    w_up = nrm(ks[18], (DEPTH, D_MODEL, D_FF), D_MODEL)
    w_down = nrm(ks[19], (DEPTH, D_FF, D_MODEL), D_FF)
    return {'x': x, 'a_norm': a_norm, 'a_w_in': a_w_in, 'a_conv_w': a_conv_w,
            'a_conv_b': a_conv_b, 'a_w_gate': a_w_gate, 'a_b_gate': a_b_gate,
            'a_lambda': a_lambda, 'a_w_out': a_w_out, 'kv_norm': kv_norm, 'w_kv': w_kv,
            'k_norm': k_norm, 'b_norm': b_norm, 'b_w_q': b_w_q, 'b_q_norm': b_q_norm,
            'b_rel_bias': b_rel_bias, 'b_w_o': b_w_o, 'mlp_norm': mlp_norm,
            'w_up': w_up, 'w_down': w_down}


def reference(x, a_norm, a_w_in, a_conv_w, a_conv_b, a_w_gate, a_b_gate, a_lambda,
              a_w_out, kv_norm, w_kv, k_norm, b_norm, b_w_q, b_q_norm, b_rel_bias,
              b_w_o, mlp_norm, w_up, w_down):
    bsz, s, _ = x.shape
    h = x
    k_pad = None
    v_pad = None
    for l in range(DEPTH):
        if l < N_A:
            h = h + recurrent_block(rmsnorm(h, a_norm[l]), a_w_in[l], a_conv_w[l], a_conv_b[l],
                                    a_w_gate[l], a_b_gate[l], a_lambda[l], a_w_out[l])
        else:
            if l == N_A:
                k_pad, v_pad = shared_kv(h, kv_norm, w_kv, k_norm)
            j = l - N_A
            q = (rmsnorm(h, b_norm[j]) @ b_w_q[j]).reshape(bsz, s, N_HEADS, HEAD_DIM)
            q = rmsnorm(q, b_q_norm[j])
            o = chunk_band_attention(q, k_pad, v_pad, b_rel_bias[j])
            h = h + o @ b_w_o[j]
        h = h + sqrelu_mlp(rmsnorm(h, mlp_norm[l]), w_up[l], w_down[l])
    return h
```

```python
import functools

import jax
import jax.numpy as jnp
from jax import lax
from jax.experimental import pallas as pl
from jax.experimental.pallas import tpu as pltpu

D_MODEL = 1024
D_RNN = D_MODEL
LRU_BLOCKS = 8
LRU_BW = D_RNN // LRU_BLOCKS
CONV_W = 4
LRU_C = 8.0
N_HEADS = 16
HEAD_DIM = 64
CHUNK = 64
LEFT_CHUNKS = 8
PAD = LEFT_CHUNKS * CHUNK
MAX_REL = 2 * CHUNK
MIN_REL = -(CHUNK - 1)
NREL = MAX_REL - MIN_REL + 1
D_FF = 4 * D_MODEL
EPS = 1e-6

LANES = 128
SUBLANES = 8
NEG = -1e30

REC_TILE = 256
MLP_TILE = 512
KVQ_TILE = 512
ATT_TILE = 512
ATT_SUB = 128
ATT_WIN = PAD + ATT_SUB
HEAD_PAIRS = N_HEADS // 2
VMEM_LIMIT = 56 * 1024 * 1024

F32 = jnp.float32
BF16 = jnp.bfloat16


def _dot(a, b):
    return jnp.dot(a, b, preferred_element_type=F32)


def _rms_scale(x):
    return lax.rsqrt(jnp.mean(x * x, axis=-1, keepdims=True) + EPS)


def _const_spec(shape):
    zeros = (0,) * len(shape)
    return pl.BlockSpec(shape, lambda *_: zeros, pipeline_mode=pl.Buffered(1))


def _rec_kernel(x_ref, norm_ref, w_in_ref, cw_ref, cb_ref, wg_ref, bg_ref, lam_ref, w_out_ref,
                o_ref, rec_ext, a_s, b_s, h_s, h_carry):
    t = REC_TILE

    @pl.when(pl.program_id(1) == 0)
    def _():
        rec_ext[0:SUBLANES, :] = jnp.zeros((SUBLANES, D_RNN), F32)
        h_carry[...] = jnp.zeros_like(h_carry)

    x = x_ref[0]
    xn = (x * _rms_scale(x) * norm_ref[...]).astype(BF16)
    u = _dot(xn, w_in_ref[...])
    gate = u[:, :D_RNN]
    rec = u[:, D_RNN:]

    rec_ext[SUBLANES:SUBLANES + t, :] = rec
    c = cb_ref[...] + rec_ext[pl.ds(SUBLANES - 3, t), :] * cw_ref[0:1, :]
    c = c + rec_ext[pl.ds(SUBLANES - 2, t), :] * cw_ref[1:2, :]
    c = c + rec_ext[pl.ds(SUBLANES - 1, t), :] * cw_ref[2:3, :]
    c = c + rec * cw_ref[3:4, :]
    rec_ext[0:SUBLANES, :] = rec_ext[t:t + SUBLANES, :]

    lam = lam_ref[...]
    neg_lam = -lam
    softplus = jnp.maximum(neg_lam, 0.0) + jnp.log1p(jnp.exp(-jnp.abs(neg_lam)))
    for n in range(LRU_BLOCKS):
        sl = slice(n * LRU_BW, (n + 1) * LRU_BW)
        cb = c[:, sl]
        g = _dot(cb.astype(BF16), wg_ref[n]) + bg_ref[n:n + 1, :]
        r = jax.nn.sigmoid(g[:, :LRU_BW])
        i = jax.nn.sigmoid(g[:, LRU_BW:])
        log_a = (-LRU_C) * r * softplus[:, sl]
        a = jnp.exp(log_a)
        mult = jnp.sqrt(1.0 - a * a)
        a_s[:, sl] = a
        b_s[:, sl] = mult * (i * cb)

    row = lax.broadcasted_iota(jnp.int32, (SUBLANES, D_RNN), 0)

    def slab(s, h_prev):
        off = pl.multiple_of(s * SUBLANES, SUBLANES)
        a = a_s[pl.ds(off, SUBLANES), :]
        b = b_s[pl.ds(off, SUBLANES), :]
        for d in (1, 2, 4):
            keep = row >= d
            a_sh = jnp.where(keep, pltpu.roll(a, d, 0), 1.0)
            b_sh = jnp.where(keep, pltpu.roll(b, d, 0), 0.0)
            b = a * b_sh + b
            a = a * a_sh
        h = a * h_prev + b
        h_s[pl.ds(off, SUBLANES), :] = h
        return h[SUBLANES - 1:SUBLANES, :]

    h_last = lax.fori_loop(0, t // SUBLANES, slab, h_carry[...], unroll=4)
    h_carry[...] = h_last

    y = (jax.nn.gelu(gate) * h_s[...]).astype(BF16)
    o_ref[0] = x + _dot(y, w_out_ref[...])


def _recurrent_block(x, norm, w_in, conv_w, conv_b, w_gate, b_gate, lam, w_out):
    bsz, s, _ = x.shape
    t = REC_TILE
    return pl.pallas_call(
        _rec_kernel,
        grid=(bsz, s // t),
        in_specs=[
            pl.BlockSpec((1, t, D_MODEL), lambda b, i: (b, i, 0)),
            _const_spec((1, D_MODEL)),
            _const_spec((D_MODEL, 2 * D_RNN)),
            _const_spec((CONV_W, D_RNN)),
            _const_spec((1, D_RNN)),
            _const_spec((LRU_BLOCKS, LRU_BW, 2 * LRU_BW)),
            _const_spec((LRU_BLOCKS, 2 * LRU_BW)),
            _const_spec((1, D_RNN)),
            _const_spec((D_RNN, D_MODEL)),
        ],
        out_specs=pl.BlockSpec((1, t, D_MODEL), lambda b, i: (b, i, 0)),
        out_shape=jax.ShapeDtypeStruct(x.shape, F32),
        scratch_shapes=[
            pltpu.VMEM((t + 2 * SUBLANES, D_RNN), F32),
            pltpu.VMEM((t, D_RNN), F32),
            pltpu.VMEM((t, D_RNN), F32),
            pltpu.VMEM((t, D_RNN), F32),
            pltpu.VMEM((1, D_RNN), F32),
        ],
        compiler_params=pltpu.CompilerParams(
            dimension_semantics=("arbitrary", "arbitrary"), vmem_limit_bytes=VMEM_LIMIT),
        name="recurrent_block",
    )(x, norm, w_in, conv_w, conv_b, w_gate, b_gate, lam, w_out)


FF_CHUNK = 1024


def _mlp_body(h, norm_ref, w_up_ref, w_down_ref):
    hn = (h * _rms_scale(h) * norm_ref[...]).astype(BF16)
    acc = h
    for j in range(D_FF // FF_CHUNK):
        sl = slice(j * FF_CHUNK, (j + 1) * FF_CHUNK)
        up = jnp.maximum(_dot(hn, w_up_ref[:, sl]), 0.0)
        acc = acc + _dot((up * up).astype(BF16), w_down_ref[sl, :])
    return acc


def _mlp_kernel(x_ref, norm_ref, w_up_ref, w_down_ref, o_ref):
    o_ref[...] = _mlp_body(x_ref[...], norm_ref, w_up_ref, w_down_ref)


def _proj_mlp_kernel(a_ref, w_o_ref, x_ref, norm_ref, w_up_ref, w_down_ref, o_ref):
    h = x_ref[...] + _dot(a_ref[...], w_o_ref[...])
    o_ref[...] = _mlp_body(h, norm_ref, w_up_ref, w_down_ref)


def _mlp(x2d, norm, w_up, w_down, attn=None, w_o=None):
    n = x2d.shape[0]
    t = MLP_TILE
    tile = pl.BlockSpec((t, D_MODEL), lambda i: (i, 0))
    specs = [tile, _const_spec((1, D_MODEL)), _const_spec((D_MODEL, D_FF)), _const_spec((D_FF, D_MODEL))]
    args = [x2d, norm, w_up, w_down]
    body = _mlp_kernel
    if attn is not None:
        specs = [tile, _const_spec((D_MODEL, D_MODEL))] + specs
        args = [attn, w_o] + args
        body = _proj_mlp_kernel
    return pl.pallas_call(
        body,
        grid=(n // t,),
        in_specs=specs,
        out_specs=tile,
        out_shape=jax.ShapeDtypeStruct(x2d.shape, F32),
        compiler_params=pltpu.CompilerParams(
            dimension_semantics=("arbitrary",), vmem_limit_bytes=VMEM_LIMIT),
        name="proj_mlp" if attn is not None else "mlp",
    )(*args)


def _head_rms_normalise(z, gain):
    col_head = lax.broadcasted_iota(jnp.int32, (D_MODEL, LANES), 0) // HEAD_DIM
    col_slot = lax.broadcasted_iota(jnp.int32, (D_MODEL, LANES), 1)
    gather = (col_head == col_slot).astype(BF16)
    row_slot = lax.broadcasted_iota(jnp.int32, (2 * LANES, D_MODEL), 0) % LANES
    row_head = lax.broadcasted_iota(jnp.int32, (2 * LANES, D_MODEL), 1) // HEAD_DIM
    spread = (row_slot == row_head).astype(BF16)
    ssq = _dot((z * z).astype(BF16), gather)
    inv = lax.rsqrt(ssq * (1.0 / HEAD_DIM) + EPS)
    hi = inv.astype(BF16)
    lo = (inv - hi.astype(F32)).astype(BF16)
    inv_b = _dot(jnp.concatenate([hi, lo], axis=1), spread)
    return z * inv_b * gain


def _kvq_kernel(x_ref, kvn_ref, qn_ref, w_kv_ref, w_q_ref, kg_ref, qg_ref, q_ref, k_ref, v_ref):
    x = x_ref[...]
    xs = x * _rms_scale(x)
    kv = _dot((xs * kvn_ref[...]).astype(BF16), w_kv_ref[...])
    v_ref[...] = kv[:, D_MODEL:].astype(BF16)
    k_ref[...] = _head_rms_normalise(kv[:, :D_MODEL], kg_ref[...]).astype(BF16)
    q = _dot((xs * qn_ref[...]).astype(BF16), w_q_ref[...])
    q_ref[...] = _head_rms_normalise(q, qg_ref[...]).astype(BF16)


def _kvq(x2d, kv_norm, q_norm, w_kv, w_q, k_gain, q_gain):
    n = x2d.shape[0]
    t = KVQ_TILE
    tile = pl.BlockSpec((t, D_MODEL), lambda i: (i, 0))
    out = jax.ShapeDtypeStruct(x2d.shape, BF16)
    return pl.pallas_call(
        _kvq_kernel,
        grid=(n // t,),
        in_specs=[tile, _const_spec((1, D_MODEL)), _const_spec((1, D_MODEL)),
                  _const_spec((D_MODEL, 2 * D_MODEL)), _const_spec((D_MODEL, D_MODEL)),
                  _const_spec((1, D_MODEL)), _const_spec((1, D_MODEL))],
        out_specs=[tile, tile, tile],
        out_shape=[out, out, out],
        compiler_params=pltpu.CompilerParams(
            dimension_semantics=("arbitrary",), vmem_limit_bytes=VMEM_LIMIT),
        name="kvq_proj",
    )(x2d, kv_norm, q_norm, w_kv, w_q, k_gain, q_gain)


BASE_W = ATT_WIN + ATT_SUB


def _bias_kernel(rb_ref, o_ref):
    rb = rb_ref[...]
    r_idx = lax.broadcasted_iota(jnp.int32, (2 * LANES, BASE_W), 0)
    m_idx = lax.broadcasted_iota(jnp.int32, (2 * LANES, BASE_W), 1)
    want = jnp.clip(PAD + ATT_SUB - m_idx, MIN_REL, MAX_REL) - MIN_REL
    onehot = (r_idx == want).astype(BF16)
    p0 = rb.astype(BF16)
    r1 = rb - p0.astype(F32)
    p1 = r1.astype(BF16)
    p2 = (r1 - p1.astype(F32)).astype(BF16)
    base = _dot(p0, onehot) + _dot(p1, onehot) + _dot(p2, onehot)

    qi = lax.broadcasted_iota(jnp.int32, (ATT_SUB, ATT_WIN), 0)
    kj = lax.broadcasted_iota(jnp.int32, (ATT_SUB, ATT_WIN), 1)
    band_lo = (qi // CHUNK) * CHUNK
    in_band = (kj >= band_lo) & (kj < band_lo + PAD + CHUNK)
    for h in range(N_HEADS):
        rows = jnp.broadcast_to(base[h:h + 1, :], (ATT_SUB, BASE_W))
        toeplitz = pltpu.roll(rows, 0, 1, stride=1, stride_axis=0)
        o_ref[h] = jnp.where(in_band, toeplitz[:, ATT_SUB:], NEG)


def _bias_table(rel_bias_padded):
    return pl.pallas_call(
        _bias_kernel,
        out_shape=jax.ShapeDtypeStruct((N_HEADS, ATT_SUB, ATT_WIN), F32),
        compiler_params=pltpu.CompilerParams(vmem_limit_bytes=VMEM_LIMIT),
        name="rel_bias_table",
    )(rel_bias_padded)


def _attn_kernel(q_ref, kp_ref, kc_ref, vp_ref, vc_ref, tbl_ref, o_ref, kcat, vcat):
    s_idx = pl.program_id(2)
    kcat[0:ATT_TILE, :] = kp_ref[0]
    kcat[ATT_TILE:, :] = kc_ref[0]
    vcat[0:ATT_TILE, :] = vp_ref[0]
    vcat[ATT_TILE:, :] = vc_ref[0]

    lane = lax.broadcasted_iota(jnp.int32, (ATT_SUB, LANES), 1)
    first_head = lane < HEAD_DIM
    kj = lax.broadcasted_iota(jnp.int32, (ATT_SUB, ATT_WIN), 1)
    pad_rows = jnp.where(s_idx == 0, ATT_TILE, 0)

    for j in range(ATT_TILE // ATT_SUB):
        r0 = j * ATT_SUB
        q = q_ref[0, r0:r0 + ATT_SUB, :]
        kwin = kcat[r0:r0 + ATT_WIN, :]
        vwin = vcat[r0:r0 + ATT_WIN, :]
        valid = kj >= pad_rows - r0
        outs = []
        for hh in range(2):
            mask = first_head if hh == 0 else jnp.logical_not(first_head)
            qm = jnp.where(mask, q, jnp.zeros_like(q))
            sc = lax.dot_general(qm, kwin, (((1,), (1,)), ((), ())), preferred_element_type=F32)
            sc = jnp.where(valid, sc + tbl_ref[hh], NEG)
            m = jnp.max(sc, axis=-1, keepdims=True)
            e = jnp.exp(sc - m)
            l = jnp.sum(e, axis=-1, keepdims=True)
            outs.append(_dot(e.astype(BF16), vwin) / l)
        o_ref[0, r0:r0 + ATT_SUB, :] = jnp.where(first_head, outs[0], outs[1]).astype(BF16)


def _attention(q, k, v, table):
    bsz, s, _ = q.shape
    t = ATT_TILE
    cur = pl.BlockSpec((1, t, LANES), lambda p, b, i: (b, i, p))
    prev = pl.BlockSpec((1, t, LANES), lambda p, b, i: (b, jnp.maximum(i - 1, 0), p))
    return pl.pallas_call(
        _attn_kernel,
        grid=(HEAD_PAIRS, bsz, s // t),
        in_specs=[cur, prev, cur, prev, cur,
                  pl.BlockSpec((2, ATT_SUB, ATT_WIN), lambda p, b, i: (p, 0, 0))],
        out_specs=cur,
        out_shape=jax.ShapeDtypeStruct(q.shape, BF16),
        scratch_shapes=[pltpu.VMEM((2 * t, LANES), BF16), pltpu.VMEM((2 * t, LANES), BF16)],
        compiler_params=pltpu.CompilerParams(
            dimension_semantics=("arbitrary", "arbitrary", "arbitrary"), vmem_limit_bytes=VMEM_LIMIT),
        name="band_attention",
    )(q, k, k, v, v, table)


def kernel(x, a_norm, a_w_in, a_conv_w, a_conv_b, a_w_gate, a_b_gate, a_lambda, a_w_out, kv_norm, w_kv, k_norm, b_norm, b_w_q, b_q_norm, b_rel_bias, b_w_o, mlp_norm, w_up, w_down):
    bsz, s, d = x.shape
    assert d == D_MODEL and s % ATT_TILE == 0 and s % REC_TILE == 0
    assert a_norm.shape[0] == 1 and b_norm.shape[0] == 1 and mlp_norm.shape[0] == 2
    n = bsz * s
    row = lambda p: p.reshape(1, -1).astype(F32)

    h = _recurrent_block(x, row(a_norm[0]), a_w_in[0].astype(BF16), a_conv_w[0], row(a_conv_b[0]),
                         a_w_gate[0].astype(BF16), a_b_gate[0], row(a_lambda[0]), a_w_out[0].astype(BF16))
    h = _mlp(h.reshape(n, d), row(mlp_norm[0]), w_up[0].astype(BF16), w_down[0].astype(BF16))

    q_gain = row(jnp.tile(b_q_norm[0], N_HEADS)) * (HEAD_DIM ** -0.5)
    k_gain = row(jnp.tile(k_norm, N_HEADS))
    q, k, v = _kvq(h, row(kv_norm), row(b_norm[0]), w_kv.astype(BF16), b_w_q[0].astype(BF16), k_gain, q_gain)
    table = _bias_table(jnp.pad(b_rel_bias[0], ((0, 0), (0, 2 * LANES - NREL))))
    shp = (bsz, s, d)
    attn = _attention(q.reshape(shp), k.reshape(shp), v.reshape(shp), table)
    out = _mlp(h, row(mlp_norm[1]), w_up[1].astype(BF16), w_down[1].astype(BF16),
               attn=attn.reshape(n, d), w_o=b_w_o[0].astype(BF16))
    return out.reshape(shp)
```

```python
import functools

import jax
import jax.numpy as jnp
from jax import lax
from jax.experimental import pallas as pl
from jax.experimental.pallas import tpu as pltpu

D_MODEL = 1024
D_RNN = D_MODEL
LRU_BLOCKS = 8
LRU_BW = D_RNN // LRU_BLOCKS
CONV_W = 4
LRU_C = 8.0
N_HEADS = 16
HEAD_DIM = 64
CHUNK = 64
LEFT_CHUNKS = 8
PAD = LEFT_CHUNKS * CHUNK
MAX_REL = 2 * CHUNK
MIN_REL = -(CHUNK - 1)
NREL = MAX_REL - MIN_REL + 1
D_FF = 4 * D_MODEL
EPS = 1e-6

LANES = 128
SUBLANES = 8
NEG = -1e30

REC_TILE = 256
MLP_TILE = 512
KVQ_TILE = 512
ATT_TILE = 1024
ATT_SUB = 256
ATT_WIN = PAD + ATT_SUB
ATT_LOOKAHEAD = 2
VT_ROWS = HEAD_DIM + 16
LOG2E = 1.4426950408889634
HEAD_PAIRS = N_HEADS // 2
VMEM_LIMIT = 56 * 1024 * 1024

F32 = jnp.float32
BF16 = jnp.bfloat16


def _dot(a, b):
    return jnp.dot(a, b, preferred_element_type=F32)


def _rms_scale(x):
    return lax.rsqrt(jnp.mean(x * x, axis=-1, keepdims=True) + EPS)


def _const_spec(shape):
    zeros = (0,) * len(shape)
    return pl.BlockSpec(shape, lambda *_: zeros, pipeline_mode=pl.Buffered(1))


def _rec_kernel(x_ref, norm_ref, w_in_ref, cw_ref, cb_ref, wg_ref, bg_ref, lam_ref, w_out_ref,
                o_ref, rec_ext, a_s, b_s, h_s, h_carry):
    t = REC_TILE

    @pl.when(pl.program_id(1) == 0)
    def _():
        rec_ext[0:SUBLANES, :] = jnp.zeros((SUBLANES, D_RNN), F32)
        h_carry[...] = jnp.zeros_like(h_carry)

    x = x_ref[0]
    xn = (x * _rms_scale(x) * norm_ref[...]).astype(BF16)
    u = _dot(xn, w_in_ref[...])
    gate = u[:, :D_RNN]
    rec = u[:, D_RNN:]

    rec_ext[SUBLANES:SUBLANES + t, :] = rec
    c = cb_ref[...] + rec_ext[pl.ds(SUBLANES - 3, t), :] * cw_ref[0:1, :]
    c = c + rec_ext[pl.ds(SUBLANES - 2, t), :] * cw_ref[1:2, :]
    c = c + rec_ext[pl.ds(SUBLANES - 1, t), :] * cw_ref[2:3, :]
    c = c + rec * cw_ref[3:4, :]
    rec_ext[0:SUBLANES, :] = rec_ext[t:t + SUBLANES, :]

    lam = lam_ref[...]
    neg_lam = -lam
    softplus = jnp.maximum(neg_lam, 0.0) + jnp.log1p(jnp.exp(-jnp.abs(neg_lam)))
    for n in range(LRU_BLOCKS):
        sl = slice(n * LRU_BW, (n + 1) * LRU_BW)
        cb = c[:, sl]
        g = _dot(cb.astype(BF16), wg_ref[n]) + bg_ref[n:n + 1, :]
        r = jax.nn.sigmoid(g[:, :LRU_BW])
        i = jax.nn.sigmoid(g[:, LRU_BW:])
        log_a = (-LRU_C) * r * softplus[:, sl]
        a = jnp.exp(log_a)
        mult = jnp.sqrt(1.0 - a * a)
        a_s[:, sl] = a
        b_s[:, sl] = mult * (i * cb)

    row = lax.broadcasted_iota(jnp.int32, (SUBLANES, D_RNN), 0)

    def slab(s, h_prev):
        off = pl.multiple_of(s * SUBLANES, SUBLANES)
        a = a_s[pl.ds(off, SUBLANES), :]
        b = b_s[pl.ds(off, SUBLANES), :]
        for d in (1, 2, 4):
            keep = row >= d
            a_sh = jnp.where(keep, pltpu.roll(a, d, 0), 1.0)
            b_sh = jnp.where(keep, pltpu.roll(b, d, 0), 0.0)
            b = a * b_sh + b
            a = a * a_sh
        h = a * h_prev + b
        h_s[pl.ds(off, SUBLANES), :] = h
        return h[SUBLANES - 1:SUBLANES, :]

    h_last = lax.fori_loop(0, t // SUBLANES, slab, h_carry[...], unroll=4)
    h_carry[...] = h_last

    y = (jax.nn.gelu(gate) * h_s[...]).astype(BF16)
    o_ref[0] = x + _dot(y, w_out_ref[...])


def _recurrent_block(x, norm, w_in, conv_w, conv_b, w_gate, b_gate, lam, w_out):
    bsz, s, _ = x.shape
    t = REC_TILE
    return pl.pallas_call(
        _rec_kernel,
        grid=(bsz, s // t),
        in_specs=[
            pl.BlockSpec((1, t, D_MODEL), lambda b, i: (b, i, 0)),
            _const_spec((1, D_MODEL)),
            _const_spec((D_MODEL, 2 * D_RNN)),
            _const_spec((CONV_W, D_RNN)),
            _const_spec((1, D_RNN)),
            _const_spec((LRU_BLOCKS, LRU_BW, 2 * LRU_BW)),
            _const_spec((LRU_BLOCKS, 2 * LRU_BW)),
            _const_spec((1, D_RNN)),
            _const_spec((D_RNN, D_MODEL)),
        ],
        out_specs=pl.BlockSpec((1, t, D_MODEL), lambda b, i: (b, i, 0)),
        out_shape=jax.ShapeDtypeStruct(x.shape, F32),
        scratch_shapes=[
            pltpu.VMEM((t + 2 * SUBLANES, D_RNN), F32),
            pltpu.VMEM((t, D_RNN), F32),
            pltpu.VMEM((t, D_RNN), F32),
            pltpu.VMEM((t, D_RNN), F32),
            pltpu.VMEM((1, D_RNN), F32),
        ],
        compiler_params=pltpu.CompilerParams(
            dimension_semantics=("arbitrary", "arbitrary"), vmem_limit_bytes=VMEM_LIMIT),
        name="recurrent_block",
    )(x, norm, w_in, conv_w, conv_b, w_gate, b_gate, lam, w_out)


FF_CHUNK = 1024


def _mlp_body(h, norm_ref, w_up_ref, w_down_ref):
    hn = (h * _rms_scale(h) * norm_ref[...]).astype(BF16)
    acc = h
    for j in range(D_FF // FF_CHUNK):
        sl = slice(j * FF_CHUNK, (j + 1) * FF_CHUNK)
        up = jnp.maximum(_dot(hn, w_up_ref[:, sl]), 0.0)
        acc = acc + _dot((up * up).astype(BF16), w_down_ref[sl, :])
    return acc


def _mlp_kernel(x_ref, norm_ref, w_up_ref, w_down_ref, o_ref):
    o_ref[...] = _mlp_body(x_ref[...], norm_ref, w_up_ref, w_down_ref)


def _proj_mlp_kernel(a_ref, w_o_ref, x_ref, norm_ref, w_up_ref, w_down_ref, o_ref):
    h = x_ref[...] + _dot(a_ref[...], w_o_ref[...])
    o_ref[...] = _mlp_body(h, norm_ref, w_up_ref, w_down_ref)


def _mlp(x2d, norm, w_up, w_down, attn=None, w_o=None):
    n = x2d.shape[0]
    t = MLP_TILE
    tile = pl.BlockSpec((t, D_MODEL), lambda i: (i, 0))
    specs = [tile, _const_spec((1, D_MODEL)), _const_spec((D_MODEL, D_FF)), _const_spec((D_FF, D_MODEL))]
    args = [x2d, norm, w_up, w_down]
    body = _mlp_kernel
    if attn is not None:
        specs = [tile, _const_spec((D_MODEL, D_MODEL))] + specs
        args = [attn, w_o] + args
        body = _proj_mlp_kernel
    return pl.pallas_call(
        body,
        grid=(n // t,),
        in_specs=specs,
        out_specs=tile,
        out_shape=jax.ShapeDtypeStruct(x2d.shape, F32),
        compiler_params=pltpu.CompilerParams(
            dimension_semantics=("arbitrary",), vmem_limit_bytes=VMEM_LIMIT),
        name="proj_mlp" if attn is not None else "mlp",
    )(*args)


def _dot_nt(a, b):
    return lax.dot_general(a, b, (((1,), (1,)), ((), ())), preferred_element_type=F32)


def _head_rms_normalise_t(zt, gain_ref):
    t = zt.shape[1]
    z3 = zt.reshape(N_HEADS, HEAD_DIM, t)
    inv = lax.rsqrt(jnp.mean(z3 * z3, axis=1, keepdims=True) + EPS)
    gain = jnp.concatenate([gain_ref[...]] * (t // LANES), axis=1)
    return (z3 * inv).reshape(N_HEADS * HEAD_DIM, t) * gain


def _kvq_kernel(x_ref, kvn_ref, qn_ref, w_kt_ref, w_vt_ref, w_qt_ref, kg_ref, qg_ref,
                qt_ref, k_ref, vt_ref):
    x = x_ref[0]
    xs = x * _rms_scale(x)
    xkv = (xs * kvn_ref[...]).astype(BF16)
    vt_ref[0] = _dot_nt(w_vt_ref[...], xkv).astype(BF16)
    kt = _head_rms_normalise_t(_dot_nt(w_kt_ref[...], xkv), kg_ref)
    k_ref[0] = kt.T.astype(BF16)
    xq = (xs * qn_ref[...]).astype(BF16)
    qt_ref[0] = _head_rms_normalise_t(_dot_nt(w_qt_ref[...], xq), qg_ref).astype(BF16)


def _kvq(x, kv_norm, q_norm, w_kt, w_vt, w_qt, k_gain, q_gain):
    bsz, s, _ = x.shape
    t = KVQ_TILE
    tok_major = pl.BlockSpec((1, t, D_MODEL), lambda b, i: (b, i, 0))
    feat_major = pl.BlockSpec((1, D_MODEL, t), lambda b, i: (b, 0, i))
    w_spec = _const_spec((D_MODEL, D_MODEL))
    g_spec = _const_spec((D_MODEL, LANES))
    ft = jax.ShapeDtypeStruct((bsz, D_MODEL, s), BF16)
    return pl.pallas_call(
        _kvq_kernel,
        grid=(bsz, s // t),
        in_specs=[tok_major, _const_spec((1, D_MODEL)), _const_spec((1, D_MODEL)),
                  w_spec, w_spec, w_spec, g_spec, g_spec],
        out_specs=[feat_major, tok_major, feat_major],
        out_shape=[ft, jax.ShapeDtypeStruct(x.shape, BF16), ft],
        compiler_params=pltpu.CompilerParams(
            dimension_semantics=("arbitrary", "arbitrary"), vmem_limit_bytes=VMEM_LIMIT),
        name="kvq_proj",
    )(x, kv_norm, q_norm, w_kt, w_vt, w_qt, k_gain, q_gain)


BASE_W = ATT_WIN + ATT_SUB
N_VARIANTS = 1 + PAD // ATT_SUB
ROLL_ROWS = 128


def _bias_kernel(rb_ref, o_ref, base_s):
    rb = rb_ref[...] * LOG2E
    r_idx = lax.broadcasted_iota(jnp.int32, (2 * LANES, BASE_W), 0)
    m_idx = lax.broadcasted_iota(jnp.int32, (2 * LANES, BASE_W), 1)
    want = jnp.clip(m_idx - ATT_SUB, MIN_REL, MAX_REL) - MIN_REL
    onehot = (r_idx == want).astype(BF16)
    p0 = rb.astype(BF16)
    r1 = rb - p0.astype(F32)
    p1 = r1.astype(BF16)
    p2 = (r1 - p1.astype(F32)).astype(BF16)
    base_s[...] = _dot(p0, onehot) + _dot(p1, onehot) + _dot(p2, onehot)
    base = base_s[pl.ds(pl.program_id(0), 1), :]

    for r0 in range(0, ATT_WIN, ROLL_ROWS):
        kj = r0 + lax.broadcasted_iota(jnp.int32, (ROLL_ROWS, ATT_SUB), 0)
        qi = lax.broadcasted_iota(jnp.int32, (ROLL_ROWS, ATT_SUB), 1)
        band_lo = (qi // CHUNK) * CHUNK
        in_band = (kj >= band_lo) & (kj < band_lo + PAD + CHUNK)
        rows = jnp.broadcast_to(base, (ROLL_ROWS, BASE_W))
        toeplitz = pltpu.roll(rows, r0, 1, stride=1, stride_axis=0)
        tile = jnp.where(in_band, toeplitz[:, ATT_WIN:], NEG)
        o_ref[0, 0, r0:r0 + ROLL_ROWS, :] = tile
        for v in range(1, N_VARIANTS):
            o_ref[v, 0, r0:r0 + ROLL_ROWS, :] = jnp.where(kj >= PAD - (v - 1) * ATT_SUB, tile, NEG)


def _bias_table(rel_bias_padded):
    return pl.pallas_call(
        _bias_kernel,
        grid=(N_HEADS,),
        in_specs=[_const_spec((N_HEADS, 2 * LANES))],
        out_specs=pl.BlockSpec((N_VARIANTS, 1, ATT_WIN, ATT_SUB), lambda h: (0, h, 0, 0)),
        out_shape=jax.ShapeDtypeStruct((N_VARIANTS, N_HEADS, ATT_WIN, ATT_SUB), F32),
        scratch_shapes=[pltpu.VMEM((N_HEADS, BASE_W), F32)],
        compiler_params=pltpu.CompilerParams(
            dimension_semantics=("arbitrary",), vmem_limit_bytes=VMEM_LIMIT),
        name="rel_bias_table",
    )(rel_bias_padded)


def _attn_kernel(qt_ref, kp_ref, kc_ref, vtp_ref, vtc_ref, tbl_ref, o_ref, kcat, vtcat, st_s, p_s):
    i = pl.program_id(2)
    kcat[0:PAD, :] = kp_ref[0]
    kcat[PAD:, :] = kc_ref[0]
    for hh in range(2):
        src = slice(hh * HEAD_DIM, (hh + 1) * HEAD_DIM)
        r0 = hh * VT_ROWS
        vtcat[r0:r0 + HEAD_DIM, 0:PAD] = vtp_ref[0, src, :]
        vtcat[r0:r0 + HEAD_DIM, PAD:] = vtc_ref[0, src, :]
        vtcat[r0 + HEAD_DIM:r0 + VT_ROWS, :] = jnp.ones((VT_ROWS - HEAD_DIM, PAD + ATT_TILE), BF16)

    feat = lax.broadcasted_iota(jnp.int32, (LANES, ATT_SUB), 0)
    first_head = feat < HEAD_DIM
    n_blk = ATT_TILE // ATT_SUB
    units = [(blk, hh) for blk in range(n_blk) for hh in range(2)]

    def scores(u):
        blk, hh = units[u]
        c0 = blk * ATT_SUB
        qt = qt_ref[0, :, c0:c0 + ATT_SUB]
        kwin = kcat[c0:c0 + ATT_WIN, :]
        variant = jnp.where(i == 0, blk + 1, 0) if blk + 1 < N_VARIANTS else 0
        mask = first_head if hh == 0 else jnp.logical_not(first_head)
        qm = jnp.where(mask, qt, jnp.zeros_like(qt))
        st = _dot(kwin, qm) + tbl_ref[variant, hh]
        st_s[u] = st
        return jnp.max(st, axis=0, keepdims=True)

    def softmax_pv(u, m):
        blk, hh = units[u]
        c0 = blk * ATT_SUB
        p_s[u] = jnp.exp2(st_s[u] - m).astype(BF16)
        vt = vtcat[hh * VT_ROWS:(hh + 1) * VT_ROWS, c0:c0 + ATT_WIN]
        ot = _dot(vt, p_s[u])
        return ot[:HEAD_DIM] * (1.0 / ot[HEAD_DIM:HEAD_DIM + 1])

    outs = []
    ms = [scores(u) for u in range(ATT_LOOKAHEAD)]
    for u in range(len(units)):
        if u + ATT_LOOKAHEAD < len(units):
            ms.append(scores(u + ATT_LOOKAHEAD))
        outs.append(softmax_pv(u, ms[u]))
        if u % 2 == 1:
            c0 = units[u][0] * ATT_SUB
            pair = jnp.concatenate(outs[-2:], axis=0)
            o_ref[0, c0:c0 + ATT_SUB, :] = pair.T.astype(BF16)


def _attention(qt, k, vt, table):
    bsz, s, _ = k.shape
    t = ATT_TILE
    prev_idx = lambda i: jnp.maximum(i * (t // PAD) - 1, 0)
    return pl.pallas_call(
        _attn_kernel,
        grid=(HEAD_PAIRS, bsz, s // t),
        in_specs=[
            pl.BlockSpec((1, LANES, t), lambda p, b, i: (b, p, i)),
            pl.BlockSpec((1, PAD, LANES), lambda p, b, i: (b, prev_idx(i), p)),
            pl.BlockSpec((1, t, LANES), lambda p, b, i: (b, i, p)),
            pl.BlockSpec((1, LANES, PAD), lambda p, b, i: (b, p, prev_idx(i))),
            pl.BlockSpec((1, LANES, t), lambda p, b, i: (b, p, i)),
            pl.BlockSpec((N_VARIANTS, 2, ATT_WIN, ATT_SUB), lambda p, b, i: (0, p, 0, 0)),
        ],
        out_specs=pl.BlockSpec((1, t, LANES), lambda p, b, i: (b, i, p)),
        out_shape=jax.ShapeDtypeStruct(k.shape, BF16),
        scratch_shapes=[pltpu.VMEM((PAD + t, LANES), BF16), pltpu.VMEM((2 * VT_ROWS, PAD + t), BF16),
                        pltpu.VMEM((2 * t // ATT_SUB, ATT_WIN, ATT_SUB), F32),
                        pltpu.VMEM((2 * t // ATT_SUB, ATT_WIN, ATT_SUB), BF16)],
        compiler_params=pltpu.CompilerParams(
            dimension_semantics=("arbitrary", "arbitrary", "arbitrary"), vmem_limit_bytes=VMEM_LIMIT),
        name="band_attention",
    )(qt, k, k, vt, vt, table)


def kernel(x, a_norm, a_w_in, a_conv_w, a_conv_b, a_w_gate, a_b_gate, a_lambda, a_w_out, kv_norm, w_kv, k_norm, b_norm, b_w_q, b_q_norm, b_rel_bias, b_w_o, mlp_norm, w_up, w_down):
    bsz, s, d = x.shape
    assert d == D_MODEL and s % ATT_TILE == 0 and s % REC_TILE == 0
    assert a_norm.shape[0] == 1 and b_norm.shape[0] == 1 and mlp_norm.shape[0] == 2
    n = bsz * s
    row = lambda p: p.reshape(1, -1).astype(F32)

    h = _recurrent_block(x, row(a_norm[0]), a_w_in[0].astype(BF16), a_conv_w[0], row(a_conv_b[0]),
                         a_w_gate[0].astype(BF16), a_b_gate[0], row(a_lambda[0]), a_w_out[0].astype(BF16))
    h = _mlp(h.reshape(n, d), row(mlp_norm[0]), w_up[0].astype(BF16), w_down[0].astype(BF16))

    col = lambda g: jnp.broadcast_to(jnp.tile(g, N_HEADS).astype(F32)[:, None], (D_MODEL, LANES))
    q_gain = col(b_q_norm[0]) * (HEAD_DIM ** -0.5 * LOG2E)
    k_gain = col(k_norm)
    shp = (bsz, s, d)
    qt, k, vt = _kvq(h.reshape(shp), row(kv_norm), row(b_norm[0]), w_kv[:, :D_MODEL].T.astype(BF16),
                     w_kv[:, D_MODEL:].T.astype(BF16), b_w_q[0].T.astype(BF16), k_gain, q_gain)
    table = _bias_table(jnp.pad(b_rel_bias[0], ((0, 0), (0, 2 * LANES - NREL))))
    attn = _attention(qt, k, vt, table)
    out = _mlp(h, row(mlp_norm[1]), w_up[1].astype(BF16), w_down[1].astype(BF16),
               attn=attn.reshape(n, d), w_o=b_w_o[0].astype(BF16))
    return out.reshape(shp)
```

```python
import functools

import jax
import jax.numpy as jnp
from jax import lax
from jax.experimental import pallas as pl
from jax.experimental.pallas import tpu as pltpu

D_MODEL = 1024
D_RNN = D_MODEL
LRU_BLOCKS = 8
LRU_BW = D_RNN // LRU_BLOCKS
CONV_W = 4
LRU_C = 8.0
N_HEADS = 16
HEAD_DIM = 64
CHUNK = 64
LEFT_CHUNKS = 8
PAD = LEFT_CHUNKS * CHUNK
MAX_REL = 2 * CHUNK
MIN_REL = -(CHUNK - 1)
NREL = MAX_REL - MIN_REL + 1
D_FF = 4 * D_MODEL
EPS = 1e-6

LANES = 128
SUBLANES = 8
NEG = -1e30

REC_TILE = 256
MLP_TILE = 512
KVQ_TILE = 512
ATT_TILE = 1024
ATT_SUB = 256
ATT_WIN = PAD + ATT_SUB
ATT_LOOKAHEAD = 2
VT_ROWS = HEAD_DIM + 16
LOG2E = 1.4426950408889634
HEAD_PAIRS = N_HEADS // 2
VMEM_LIMIT = 56 * 1024 * 1024

F32 = jnp.float32
BF16 = jnp.bfloat16


def _dot(a, b):
    return jnp.dot(a, b, preferred_element_type=F32)


def _rms_scale(x):
    return lax.rsqrt(jnp.mean(x * x, axis=-1, keepdims=True) + EPS)


def _const_spec(shape):
    zeros = (0,) * len(shape)
    return pl.BlockSpec(shape, lambda *_: zeros, pipeline_mode=pl.Buffered(1))


def _layer0_kernel(x_ref, norm_ref, w_in_ref, cw_ref, cb_ref, wg_ref, bg_ref, lam_ref, w_out_ref,
                   mnorm_ref, w_up_ref, w_down_ref, o_ref,
                   rec_ext, a_s, b_s, h_s, h_carry, mix_s, *, tiles_per_seq):
    t = REC_TILE
    j = pl.program_id(0)

    @pl.when(j % tiles_per_seq == 0)
    def _():
        rec_ext[0:SUBLANES, :] = jnp.zeros((SUBLANES, D_RNN), F32)
        h_carry[...] = jnp.zeros_like(h_carry)

    @pl.when(j == 0)
    def _():
        mix_s[1] = jnp.zeros((t, D_MODEL), F32)

    slot = j % 2

    hp = mix_s[1 - slot]
    hn = (hp * _rms_scale(hp) * mnorm_ref[...]).astype(BF16)

    def mlp_chunk(k, acc):
        sl = slice(k * FF_CHUNK, (k + 1) * FF_CHUNK)
        up = jnp.maximum(_dot(hn, w_up_ref[:, sl]), 0.0)
        return acc + _dot((up * up).astype(BF16), w_down_ref[sl, :])

    x = x_ref[...]
    xn = (x * _rms_scale(x) * norm_ref[...]).astype(BF16)
    u = _dot(xn, w_in_ref[...])
    gate = u[:, :D_RNN]
    rec = u[:, D_RNN:]

    acc = mlp_chunk(0, hp)

    rec_ext[SUBLANES:SUBLANES + t, :] = rec
    c = cb_ref[...] + rec_ext[pl.ds(SUBLANES - 3, t), :] * cw_ref[0:1, :]
    c = c + rec_ext[pl.ds(SUBLANES - 2, t), :] * cw_ref[1:2, :]
    c = c + rec_ext[pl.ds(SUBLANES - 1, t), :] * cw_ref[2:3, :]
    c = c + rec * cw_ref[3:4, :]
    rec_ext[0:SUBLANES, :] = rec_ext[t:t + SUBLANES, :]

    lam = lam_ref[...]
    neg_lam = -lam
    softplus = jnp.maximum(neg_lam, 0.0) + jnp.log1p(jnp.exp(-jnp.abs(neg_lam)))
    for n in range(LRU_BLOCKS):
        sl = slice(n * LRU_BW, (n + 1) * LRU_BW)
        cb = c[:, sl]
        g = _dot(cb.astype(BF16), wg_ref[n]) + bg_ref[n:n + 1, :]
        r = jax.nn.sigmoid(g[:, :LRU_BW])
        i = jax.nn.sigmoid(g[:, LRU_BW:])
        log_a = (-LRU_C) * r * softplus[:, sl]
        a = jnp.exp(log_a)
        mult = jnp.sqrt(1.0 - a * a)
        a_s[:, sl] = a
        b_s[:, sl] = mult * (i * cb)

    acc = mlp_chunk(1, acc)

    row = lax.broadcasted_iota(jnp.int32, (SUBLANES, D_RNN), 0)

    def slab(s, h_prev):
        rows = slice(s * SUBLANES, (s + 1) * SUBLANES)
        a = a_s[rows, :]
        b = b_s[rows, :]
        for d in (1, 2, 4):
            keep = row >= d
            a_sh = jnp.where(keep, pltpu.roll(a, d, 0), 1.0)
            b_sh = jnp.where(keep, pltpu.roll(b, d, 0), 0.0)
            b = a * b_sh + b
            a = a * a_sh
        h = a * h_prev + b
        h_s[rows, :] = h
        return h[SUBLANES - 1:SUBLANES, :]

    n_slabs = t // SUBLANES
    h_last = h_carry[...]
    for s in range(n_slabs // 2):
        h_last = slab(s, h_last)
    acc = mlp_chunk(2, acc)
    for s in range(n_slabs // 2, n_slabs):
        h_last = slab(s, h_last)
    h_carry[...] = h_last
    acc = mlp_chunk(3, acc)

    y = (jax.nn.gelu(gate) * h_s[...]).astype(BF16)
    mix_s[slot] = x + _dot(y, w_out_ref[...])
    o_ref[...] = acc


def _layer0(x2d, tiles_per_seq, norm, w_in, conv_w, conv_b, w_gate, b_gate, lam, w_out, mlp_norm, w_up, w_down):
    n_tiles = x2d.shape[0] // REC_TILE
    t = REC_TILE
    assert D_FF // FF_CHUNK == 4
    return pl.pallas_call(
        functools.partial(_layer0_kernel, tiles_per_seq=tiles_per_seq),
        grid=(n_tiles + 1,),
        in_specs=[
            pl.BlockSpec((t, D_MODEL), lambda j: (jnp.minimum(j, n_tiles - 1), 0)),
            _const_spec((1, D_MODEL)),
            _const_spec((D_MODEL, 2 * D_RNN)),
            _const_spec((CONV_W, D_RNN)),
            _const_spec((1, D_RNN)),
            _const_spec((LRU_BLOCKS, LRU_BW, 2 * LRU_BW)),
            _const_spec((LRU_BLOCKS, 2 * LRU_BW)),
            _const_spec((1, D_RNN)),
            _const_spec((D_RNN, D_MODEL)),
            _const_spec((1, D_MODEL)),
            _const_spec((D_MODEL, D_FF)),
            _const_spec((D_FF, D_MODEL)),
        ],
        out_specs=pl.BlockSpec((t, D_MODEL), lambda j: (jnp.maximum(j - 1, 0), 0)),
        out_shape=jax.ShapeDtypeStruct(x2d.shape, F32),
        scratch_shapes=[
            pltpu.VMEM((t + SUBLANES, D_RNN), F32),
            pltpu.VMEM((t, D_RNN), F32),
            pltpu.VMEM((t, D_RNN), F32),
            pltpu.VMEM((t, D_RNN), F32),
            pltpu.VMEM((1, D_RNN), F32),
            pltpu.VMEM((2, t, D_MODEL), F32),
        ],
        compiler_params=pltpu.CompilerParams(
            dimension_semantics=("arbitrary",), vmem_limit_bytes=VMEM_LIMIT),
        name="layer0_mixer_mlp",
    )(x2d, norm, w_in, conv_w, conv_b, w_gate, b_gate, lam, w_out, mlp_norm, w_up, w_down)


FF_CHUNK = 1024


def _mlp_body(h, norm_ref, w_up_ref, w_down_ref):
    hn = (h * _rms_scale(h) * norm_ref[...]).astype(BF16)
    acc = h
    for j in range(D_FF // FF_CHUNK):
        sl = slice(j * FF_CHUNK, (j + 1) * FF_CHUNK)
        up = jnp.maximum(_dot(hn, w_up_ref[:, sl]), 0.0)
        acc = acc + _dot((up * up).astype(BF16), w_down_ref[sl, :])
    return acc


def _proj_mlp_kernel(a_ref, w_o_ref, x_ref, norm_ref, w_up_ref, w_down_ref, o_ref):
    h = x_ref[...] + _dot(a_ref[...], w_o_ref[...])
    o_ref[...] = _mlp_body(h, norm_ref, w_up_ref, w_down_ref)


def _proj_mlp(attn, w_o, x2d, norm, w_up, w_down):
    n = x2d.shape[0]
    t = MLP_TILE
    tile = pl.BlockSpec((t, D_MODEL), lambda i: (i, 0))
    return pl.pallas_call(
        _proj_mlp_kernel,
        grid=(n // t,),
        in_specs=[tile, _const_spec((D_MODEL, D_MODEL)), tile, _const_spec((1, D_MODEL)),
                  _const_spec((D_MODEL, D_FF)), _const_spec((D_FF, D_MODEL))],
        out_specs=tile,
        out_shape=jax.ShapeDtypeStruct(x2d.shape, F32),
        compiler_params=pltpu.CompilerParams(
            dimension_semantics=("arbitrary",), vmem_limit_bytes=VMEM_LIMIT),
        name="proj_mlp",
    )(attn, w_o, x2d, norm, w_up, w_down)


def _dot_nt(a, b):
    return lax.dot_general(a, b, (((1,), (1,)), ((), ())), preferred_element_type=F32)


def _head_rms_normalise_t(zt, gain_ref):
    t = zt.shape[1]
    z3 = zt.reshape(N_HEADS, HEAD_DIM, t)
    inv = lax.rsqrt(jnp.mean(z3 * z3, axis=1, keepdims=True) + EPS)
    gain = jnp.concatenate([gain_ref[...]] * (t // LANES), axis=1)
    return (z3 * inv).reshape(N_HEADS * HEAD_DIM, t) * gain


def _kvq_kernel(x_ref, kvn_ref, qn_ref, w_kt_ref, w_vt_ref, w_qt_ref, kg_ref, qg_ref,
                qt_ref, k_ref, vt_ref):
    x = x_ref[0]
    xs = x * _rms_scale(x)
    xkv = (xs * kvn_ref[...]).astype(BF16)
    vt_ref[0] = _dot_nt(w_vt_ref[...], xkv).astype(BF16)
    kt = _head_rms_normalise_t(_dot_nt(w_kt_ref[...], xkv), kg_ref)
    k_ref[0] = kt.T.astype(BF16)
    xq = (xs * qn_ref[...]).astype(BF16)
    qt_ref[0] = _head_rms_normalise_t(_dot_nt(w_qt_ref[...], xq), qg_ref).astype(BF16)


def _kvq(x, kv_norm, q_norm, w_kt, w_vt, w_qt, k_gain, q_gain):
    bsz, s, _ = x.shape
    t = KVQ_TILE
    tok_major = pl.BlockSpec((1, t, D_MODEL), lambda b, i: (b, i, 0))
    feat_major = pl.BlockSpec((1, D_MODEL, t), lambda b, i: (b, 0, i))
    w_spec = _const_spec((D_MODEL, D_MODEL))
    g_spec = _const_spec((D_MODEL, LANES))
    ft = jax.ShapeDtypeStruct((bsz, D_MODEL, s), BF16)
    return pl.pallas_call(
        _kvq_kernel,
        grid=(bsz, s // t),
        in_specs=[tok_major, _const_spec((1, D_MODEL)), _const_spec((1, D_MODEL)),
                  w_spec, w_spec, w_spec, g_spec, g_spec],
        out_specs=[feat_major, tok_major, feat_major],
        out_shape=[ft, jax.ShapeDtypeStruct(x.shape, BF16), ft],
        compiler_params=pltpu.CompilerParams(
            dimension_semantics=("arbitrary", "arbitrary"), vmem_limit_bytes=VMEM_LIMIT),
        name="kvq_proj",
    )(x, kv_norm, q_norm, w_kt, w_vt, w_qt, k_gain, q_gain)


BASE_W = ATT_WIN + ATT_SUB
N_VARIANTS = 1 + PAD // ATT_SUB
ROLL_ROWS = 128


def _bias_kernel(rb_ref, o_ref, base_s):
    rb = rb_ref[...] * LOG2E
    r_idx = lax.broadcasted_iota(jnp.int32, (2 * LANES, BASE_W), 0)
    m_idx = lax.broadcasted_iota(jnp.int32, (2 * LANES, BASE_W), 1)
    want = jnp.clip(m_idx - ATT_SUB, MIN_REL, MAX_REL) - MIN_REL
    onehot = (r_idx == want).astype(BF16)
    p0 = rb.astype(BF16)
    r1 = rb - p0.astype(F32)
    p1 = r1.astype(BF16)
    p2 = (r1 - p1.astype(F32)).astype(BF16)
    base_s[...] = _dot(p0, onehot) + _dot(p1, onehot) + _dot(p2, onehot)
    base = base_s[pl.ds(pl.program_id(0), 1), :]

    for r0 in range(0, ATT_WIN, ROLL_ROWS):
        kj = r0 + lax.broadcasted_iota(jnp.int32, (ROLL_ROWS, ATT_SUB), 0)
        qi = lax.broadcasted_iota(jnp.int32, (ROLL_ROWS, ATT_SUB), 1)
        band_lo = (qi // CHUNK) * CHUNK
        in_band = (kj >= band_lo) & (kj < band_lo + PAD + CHUNK)
        rows = jnp.broadcast_to(base, (ROLL_ROWS, BASE_W))
        toeplitz = pltpu.roll(rows, r0, 1, stride=1, stride_axis=0)
        tile = jnp.where(in_band, toeplitz[:, ATT_WIN:], NEG)
        o_ref[0, 0, r0:r0 + ROLL_ROWS, :] = tile
        for v in range(1, N_VARIANTS):
            o_ref[v, 0, r0:r0 + ROLL_ROWS, :] = jnp.where(kj >= PAD - (v - 1) * ATT_SUB, tile, NEG)


def _bias_table(rel_bias_padded):
    return pl.pallas_call(
        _bias_kernel,
        grid=(N_HEADS,),
        in_specs=[_const_spec((N_HEADS, 2 * LANES))],
        out_specs=pl.BlockSpec((N_VARIANTS, 1, ATT_WIN, ATT_SUB), lambda h: (0, h, 0, 0)),
        out_shape=jax.ShapeDtypeStruct((N_VARIANTS, N_HEADS, ATT_WIN, ATT_SUB), F32),
        scratch_shapes=[pltpu.VMEM((N_HEADS, BASE_W), F32)],
        compiler_params=pltpu.CompilerParams(
            dimension_semantics=("arbitrary",), vmem_limit_bytes=VMEM_LIMIT),
        name="rel_bias_table",
    )(rel_bias_padded)


def _attn_kernel(qt_ref, kp_ref, kc_ref, vtp_ref, vtc_ref, tbl_ref, o_ref, kcat, vtcat, st_s, p_s):
    i = pl.program_id(2)
    kcat[0:PAD, :] = kp_ref[0]
    kcat[PAD:, :] = kc_ref[0]
    for hh in range(2):
        src = slice(hh * HEAD_DIM, (hh + 1) * HEAD_DIM)
        r0 = hh * VT_ROWS
        vtcat[r0:r0 + HEAD_DIM, 0:PAD] = vtp_ref[0, src, :]
        vtcat[r0:r0 + HEAD_DIM, PAD:] = vtc_ref[0, src, :]
        vtcat[r0 + HEAD_DIM:r0 + VT_ROWS, :] = jnp.ones((VT_ROWS - HEAD_DIM, PAD + ATT_TILE), BF16)

    feat = lax.broadcasted_iota(jnp.int32, (LANES, ATT_SUB), 0)
    first_head = feat < HEAD_DIM
    n_blk = ATT_TILE // ATT_SUB
    units = [(blk, hh) for blk in range(n_blk) for hh in range(2)]

    def scores(u):
        blk, hh = units[u]
        c0 = blk * ATT_SUB
        qt = qt_ref[0, :, c0:c0 + ATT_SUB]
        kwin = kcat[c0:c0 + ATT_WIN, :]
        variant = jnp.where(i == 0, blk + 1, 0) if blk + 1 < N_VARIANTS else 0
        mask = first_head if hh == 0 else jnp.logical_not(first_head)
        qm = jnp.where(mask, qt, jnp.zeros_like(qt))
        st = _dot(kwin, qm) + tbl_ref[variant, hh]
        st_s[u] = st
        return jnp.max(st, axis=0, keepdims=True)

    def softmax_pv(u, m):
        blk, hh = units[u]
        c0 = blk * ATT_SUB
        p_s[u] = jnp.exp2(st_s[u] - m).astype(BF16)
        vt = vtcat[hh * VT_ROWS:(hh + 1) * VT_ROWS, c0:c0 + ATT_WIN]
        ot = _dot(vt, p_s[u])
        return ot[:HEAD_DIM] * (1.0 / ot[HEAD_DIM:HEAD_DIM + 1])

    outs = []
    ms = [scores(u) for u in range(ATT_LOOKAHEAD)]
    for u in range(len(units)):
        if u + ATT_LOOKAHEAD < len(units):
            ms.append(scores(u + ATT_LOOKAHEAD))
        outs.append(softmax_pv(u, ms[u]))
        if u % 2 == 1:
            c0 = units[u][0] * ATT_SUB
            pair = jnp.concatenate(outs[-2:], axis=0)
            o_ref[0, c0:c0 + ATT_SUB, :] = pair.T.astype(BF16)


def _attention(qt, k, vt, table):
    bsz, s, _ = k.shape
    t = ATT_TILE
    prev_idx = lambda i: jnp.maximum(i * (t // PAD) - 1, 0)
    return pl.pallas_call(
        _attn_kernel,
        grid=(HEAD_PAIRS, bsz, s // t),
        in_specs=[
            pl.BlockSpec((1, LANES, t), lambda p, b, i: (b, p, i)),
            pl.BlockSpec((1, PAD, LANES), lambda p, b, i: (b, prev_idx(i), p)),
            pl.BlockSpec((1, t, LANES), lambda p, b, i: (b, i, p)),
            pl.BlockSpec((1, LANES, PAD), lambda p, b, i: (b, p, prev_idx(i))),
            pl.BlockSpec((1, LANES, t), lambda p, b, i: (b, p, i)),
            pl.BlockSpec((N_VARIANTS, 2, ATT_WIN, ATT_SUB), lambda p, b, i: (0, p, 0, 0)),
        ],
        out_specs=pl.BlockSpec((1, t, LANES), lambda p, b, i: (b, i, p)),
        out_shape=jax.ShapeDtypeStruct(k.shape, BF16),
        scratch_shapes=[pltpu.VMEM((PAD + t, LANES), BF16), pltpu.VMEM((2 * VT_ROWS, PAD + t), BF16),
                        pltpu.VMEM((2 * t // ATT_SUB, ATT_WIN, ATT_SUB), F32),
                        pltpu.VMEM((2 * t // ATT_SUB, ATT_WIN, ATT_SUB), BF16)],
        compiler_params=pltpu.CompilerParams(
            dimension_semantics=("arbitrary", "arbitrary", "arbitrary"), vmem_limit_bytes=VMEM_LIMIT),
        name="band_attention",
    )(qt, k, k, vt, vt, table)


def kernel(x, a_norm, a_w_in, a_conv_w, a_conv_b, a_w_gate, a_b_gate, a_lambda, a_w_out, kv_norm, w_kv, k_norm, b_norm, b_w_q, b_q_norm, b_rel_bias, b_w_o, mlp_norm, w_up, w_down):
    bsz, s, d = x.shape
    assert d == D_MODEL and s % ATT_TILE == 0 and s % REC_TILE == 0
    assert a_norm.shape[0] == 1 and b_norm.shape[0] == 1 and mlp_norm.shape[0] == 2
    n = bsz * s
    row = lambda p: p.reshape(1, -1).astype(F32)

    h = _layer0(x.reshape(n, d), s // REC_TILE, row(a_norm[0]), a_w_in[0].astype(BF16), a_conv_w[0],
                row(a_conv_b[0]), a_w_gate[0].astype(BF16), a_b_gate[0], row(a_lambda[0]),
                a_w_out[0].astype(BF16), row(mlp_norm[0]), w_up[0].astype(BF16), w_down[0].astype(BF16))

    col = lambda g: jnp.broadcast_to(jnp.tile(g, N_HEADS).astype(F32)[:, None], (D_MODEL, LANES))
    q_gain = col(b_q_norm[0]) * (HEAD_DIM ** -0.5 * LOG2E)
    k_gain = col(k_norm)
    shp = (bsz, s, d)
    qt, k, vt = _kvq(h.reshape(shp), row(kv_norm), row(b_norm[0]), w_kv[:, :D_MODEL].T.astype(BF16),
                     w_kv[:, D_MODEL:].T.astype(BF16), b_w_q[0].T.astype(BF16), k_gain, q_gain)
    table = _bias_table(jnp.pad(b_rel_bias[0], ((0, 0), (0, 2 * LANES - NREL))))
    attn = _attention(qt, k, vt, table)
    out = _proj_mlp(attn.reshape(n, d), b_w_o[0].astype(BF16), h, row(mlp_norm[1]),
                    w_up[1].astype(BF16), w_down[1].astype(BF16))
    return out.reshape(shp)
```

```python
import functools

import jax
import jax.numpy as jnp
from jax import lax
from jax.experimental import pallas as pl
from jax.experimental.pallas import tpu as pltpu

D_MODEL = 1024
D_RNN = D_MODEL
LRU_BLOCKS = 8
LRU_BW = D_RNN // LRU_BLOCKS
CONV_W = 4
LRU_C = 8.0
N_HEADS = 16
HEAD_DIM = 64
CHUNK = 64
LEFT_CHUNKS = 8
PAD = LEFT_CHUNKS * CHUNK
MAX_REL = 2 * CHUNK
MIN_REL = -(CHUNK - 1)
NREL = MAX_REL - MIN_REL + 1
D_FF = 4 * D_MODEL
EPS = 1e-6

LANES = 128
SUBLANES = 8
NEG = -1e30

REC_TILE = 256
MLP_TILE = 1024
KVQ_TILE = 512
ATT_TILE = 4096
ATT_SUB = 256
ATT_WIN = PAD + ATT_SUB
ATT_LOOKAHEAD = 4
ATT_SLOTS = ATT_LOOKAHEAD + 2
VT_ROWS = HEAD_DIM + 16
LOG2E = 1.4426950408889634
HEAD_PAIRS = N_HEADS // 2
VMEM_LIMIT = 56 * 1024 * 1024

F32 = jnp.float32
BF16 = jnp.bfloat16


def _dot(a, b):
    return jnp.dot(a, b, preferred_element_type=F32)


def _rms_scale(x):
    return lax.rsqrt(jnp.mean(x * x, axis=-1, keepdims=True) + EPS)


def _const_spec(shape):
    zeros = (0,) * len(shape)
    return pl.BlockSpec(shape, lambda *_: zeros, pipeline_mode=pl.Buffered(1))


def _layer0_kernel(x_ref, norm_ref, w_in_ref, cw_ref, cb_ref, wg_ref, bg_ref, lam_ref, w_out_ref,
                   mnorm_ref, w_up_ref, w_down_ref, o_ref,
                   rec_ext, a_s, b_s, h_s, h_carry, mix_s, *, tiles_per_seq):
    t = REC_TILE
    j = pl.program_id(0)

    @pl.when(j % tiles_per_seq == 0)
    def _():
        rec_ext[0:SUBLANES, :] = jnp.zeros((SUBLANES, D_RNN), F32)
        h_carry[...] = jnp.zeros_like(h_carry)

    @pl.when(j == 0)
    def _():
        mix_s[1] = jnp.zeros((t, D_MODEL), F32)

    slot = j % 2

    hp = mix_s[1 - slot]
    hn = (hp * _rms_scale(hp) * mnorm_ref[...]).astype(BF16)

    def mlp_chunk(k, acc):
        sl = slice(k * FF_CHUNK, (k + 1) * FF_CHUNK)
        up = jnp.maximum(_dot(hn, w_up_ref[:, sl]), 0.0)
        return acc + _dot((up * up).astype(BF16), w_down_ref[sl, :])

    x = x_ref[...]
    xn = (x * _rms_scale(x) * norm_ref[...]).astype(BF16)
    u = _dot(xn, w_in_ref[...])
    gate = u[:, :D_RNN]
    rec = u[:, D_RNN:]

    acc = mlp_chunk(0, hp)

    rec_ext[SUBLANES:SUBLANES + t, :] = rec
    c = cb_ref[...] + rec_ext[pl.ds(SUBLANES - 3, t), :] * cw_ref[0:1, :]
    c = c + rec_ext[pl.ds(SUBLANES - 2, t), :] * cw_ref[1:2, :]
    c = c + rec_ext[pl.ds(SUBLANES - 1, t), :] * cw_ref[2:3, :]
    c = c + rec * cw_ref[3:4, :]
    rec_ext[0:SUBLANES, :] = rec_ext[t:t + SUBLANES, :]

    lam = lam_ref[...]
    neg_lam = -lam
    softplus = jnp.maximum(neg_lam, 0.0) + jnp.log1p(jnp.exp(-jnp.abs(neg_lam)))
    for n in range(LRU_BLOCKS):
        sl = slice(n * LRU_BW, (n + 1) * LRU_BW)
        cb = c[:, sl]
        g = _dot(cb.astype(BF16), wg_ref[n]) + bg_ref[n:n + 1, :]
        r = jax.nn.sigmoid(g[:, :LRU_BW])
        i = jax.nn.sigmoid(g[:, LRU_BW:])
        log_a = (-LRU_C) * r * softplus[:, sl]
        a = jnp.exp(log_a)
        mult = jnp.sqrt(1.0 - a * a)
        a_s[:, sl] = a
        b_s[:, sl] = mult * (i * cb)

    acc = mlp_chunk(1, acc)

    row = lax.broadcasted_iota(jnp.int32, (SUBLANES, D_RNN), 0)

    def slab(s, h_prev):
        rows = slice(s * SUBLANES, (s + 1) * SUBLANES)
        a = a_s[rows, :]
        b = b_s[rows, :]
        for d in (1, 2, 4):
            keep = row >= d
            a_sh = jnp.where(keep, pltpu.roll(a, d, 0), 1.0)
            b_sh = jnp.where(keep, pltpu.roll(b, d, 0), 0.0)
            b = a * b_sh + b
            a = a * a_sh
        h = a * h_prev + b
        h_s[rows, :] = h
        return h[SUBLANES - 1:SUBLANES, :]

    n_slabs = t // SUBLANES
    h_last = h_carry[...]
    for s in range(n_slabs // 2):
        h_last = slab(s, h_last)
    acc = mlp_chunk(2, acc)
    for s in range(n_slabs // 2, n_slabs):
        h_last = slab(s, h_last)
    h_carry[...] = h_last
    acc = mlp_chunk(3, acc)

    y = (jax.nn.gelu(gate) * h_s[...]).astype(BF16)
    mix_s[slot] = x + _dot(y, w_out_ref[...])
    o_ref[...] = acc


def _layer0(x2d, tiles_per_seq, norm, w_in, conv_w, conv_b, w_gate, b_gate, lam, w_out, mlp_norm, w_up, w_down):
    n_tiles = x2d.shape[0] // REC_TILE
    t = REC_TILE
    assert D_FF // FF_CHUNK == 4
    return pl.pallas_call(
        functools.partial(_layer0_kernel, tiles_per_seq=tiles_per_seq),
        grid=(n_tiles + 1,),
        in_specs=[
            pl.BlockSpec((t, D_MODEL), lambda j: (jnp.minimum(j, n_tiles - 1), 0)),
            _const_spec((1, D_MODEL)),
            _const_spec((D_MODEL, 2 * D_RNN)),
            _const_spec((CONV_W, D_RNN)),
            _const_spec((1, D_RNN)),
            _const_spec((LRU_BLOCKS, LRU_BW, 2 * LRU_BW)),
            _const_spec((LRU_BLOCKS, 2 * LRU_BW)),
            _const_spec((1, D_RNN)),
            _const_spec((D_RNN, D_MODEL)),
            _const_spec((1, D_MODEL)),
            _const_spec((D_MODEL, D_FF)),
            _const_spec((D_FF, D_MODEL)),
        ],
        out_specs=pl.BlockSpec((t, D_MODEL), lambda j: (jnp.maximum(j - 1, 0), 0)),
        out_shape=jax.ShapeDtypeStruct(x2d.shape, F32),
        scratch_shapes=[
            pltpu.VMEM((t + SUBLANES, D_RNN), F32),
            pltpu.VMEM((t, D_RNN), F32),
            pltpu.VMEM((t, D_RNN), F32),
            pltpu.VMEM((t, D_RNN), F32),
            pltpu.VMEM((1, D_RNN), F32),
            pltpu.VMEM((2, t, D_MODEL), F32),
        ],
        compiler_params=pltpu.CompilerParams(
            dimension_semantics=("arbitrary",), vmem_limit_bytes=VMEM_LIMIT),
        name="layer0_mixer_mlp",
    )(x2d, norm, w_in, conv_w, conv_b, w_gate, b_gate, lam, w_out, mlp_norm, w_up, w_down)


FF_CHUNK = 1024


def _mlp_body(h, norm_ref, w_up_ref, w_down_ref):
    hn = (h * _rms_scale(h) * norm_ref[...]).astype(BF16)
    acc = h
    for j in range(D_FF // FF_CHUNK):
        sl = slice(j * FF_CHUNK, (j + 1) * FF_CHUNK)
        up = jnp.maximum(_dot(hn, w_up_ref[:, sl]), 0.0)
        acc = acc + _dot((up * up).astype(BF16), w_down_ref[sl, :])
    return acc


def _proj_mlp_kernel(a_ref, w_o_ref, x_ref, norm_ref, w_up_ref, w_down_ref, o_ref):
    h = x_ref[...] + _dot(a_ref[...], w_o_ref[...])
    o_ref[...] = _mlp_body(h, norm_ref, w_up_ref, w_down_ref)


def _proj_mlp(attn, w_o, x2d, norm, w_up, w_down):
    n = x2d.shape[0]
    t = MLP_TILE
    tile = pl.BlockSpec((t, D_MODEL), lambda i: (i, 0))
    return pl.pallas_call(
        _proj_mlp_kernel,
        grid=(n // t,),
        in_specs=[tile, _const_spec((D_MODEL, D_MODEL)), tile, _const_spec((1, D_MODEL)),
                  _const_spec((D_MODEL, D_FF)), _const_spec((D_FF, D_MODEL))],
        out_specs=tile,
        out_shape=jax.ShapeDtypeStruct(x2d.shape, F32),
        compiler_params=pltpu.CompilerParams(
            dimension_semantics=("arbitrary",), vmem_limit_bytes=VMEM_LIMIT),
        name="proj_mlp",
    )(attn, w_o, x2d, norm, w_up, w_down)


def _dot_nt(a, b):
    return lax.dot_general(a, b, (((1,), (1,)), ((), ())), preferred_element_type=F32)


def _head_rms_normalise_t(zt, gain_ref):
    t = zt.shape[1]
    z3 = zt.reshape(N_HEADS, HEAD_DIM, t)
    inv = lax.rsqrt(jnp.mean(z3 * z3, axis=1, keepdims=True) + EPS)
    gain = jnp.concatenate([gain_ref[...]] * (t // LANES), axis=1)
    return (z3 * inv).reshape(N_HEADS * HEAD_DIM, t) * gain


def _kvq_kernel(x_ref, kvn_ref, qn_ref, w_kt_ref, w_vt_ref, w_qt_ref, kg_ref, qg_ref,
                qt_ref, k_ref, vt_ref):
    x = x_ref[0]
    xs = x * _rms_scale(x)
    xkv = (xs * kvn_ref[...]).astype(BF16)
    vt_ref[0] = _dot_nt(w_vt_ref[...], xkv).astype(BF16)
    kt = _head_rms_normalise_t(_dot_nt(w_kt_ref[...], xkv), kg_ref)
    k_ref[0] = kt.T.astype(BF16)
    xq = (xs * qn_ref[...]).astype(BF16)
    qt_ref[0] = _head_rms_normalise_t(_dot_nt(w_qt_ref[...], xq), qg_ref).astype(BF16)


def _kvq(x, kv_norm, q_norm, w_kt, w_vt, w_qt, k_gain, q_gain):
    bsz, s, _ = x.shape
    t = KVQ_TILE
    tok_major = pl.BlockSpec((1, t, D_MODEL), lambda b, i: (b, i, 0))
    feat_major = pl.BlockSpec((1, D_MODEL, t), lambda b, i: (b, 0, i))
    w_spec = _const_spec((D_MODEL, D_MODEL))
    g_spec = _const_spec((D_MODEL, LANES))
    ft = jax.ShapeDtypeStruct((bsz, D_MODEL, s), BF16)
    return pl.pallas_call(
        _kvq_kernel,
        grid=(bsz, s // t),
        in_specs=[tok_major, _const_spec((1, D_MODEL)), _const_spec((1, D_MODEL)),
                  w_spec, w_spec, w_spec, g_spec, g_spec],
        out_specs=[feat_major, tok_major, feat_major],
        out_shape=[ft, jax.ShapeDtypeStruct(x.shape, BF16), ft],
        compiler_params=pltpu.CompilerParams(
            dimension_semantics=("arbitrary", "arbitrary"), vmem_limit_bytes=VMEM_LIMIT),
        name="kvq_proj",
    )(x, kv_norm, q_norm, w_kt, w_vt, w_qt, k_gain, q_gain)


BASE_W = ATT_WIN + ATT_SUB
N_VARIANTS = 1 + PAD // ATT_SUB
ROLL_ROWS = 128


def _bias_kernel(rb_ref, o_ref, base_s):
    rb = rb_ref[...] * LOG2E
    r_idx = lax.broadcasted_iota(jnp.int32, (2 * LANES, BASE_W), 0)
    m_idx = lax.broadcasted_iota(jnp.int32, (2 * LANES, BASE_W), 1)
    want = jnp.clip(m_idx - ATT_SUB, MIN_REL, MAX_REL) - MIN_REL
    onehot = (r_idx == want).astype(BF16)
    p0 = rb.astype(BF16)
    r1 = rb - p0.astype(F32)
    p1 = r1.astype(BF16)
    p2 = (r1 - p1.astype(F32)).astype(BF16)
    base_s[...] = _dot(p0, onehot) + _dot(p1, onehot) + _dot(p2, onehot)
    base = base_s[pl.ds(pl.program_id(0), 1), :]

    for r0 in range(0, ATT_WIN, ROLL_ROWS):
        kj = r0 + lax.broadcasted_iota(jnp.int32, (ROLL_ROWS, ATT_SUB), 0)
        qi = lax.broadcasted_iota(jnp.int32, (ROLL_ROWS, ATT_SUB), 1)
        band_lo = (qi // CHUNK) * CHUNK
        in_band = (kj >= band_lo) & (kj < band_lo + PAD + CHUNK)
        rows = jnp.broadcast_to(base, (ROLL_ROWS, BASE_W))
        toeplitz = pltpu.roll(rows, r0, 1, stride=1, stride_axis=0)
        tile = jnp.where(in_band, toeplitz[:, ATT_WIN:], NEG)
        o_ref[0, 0, r0:r0 + ROLL_ROWS, :] = tile
        for v in range(1, N_VARIANTS):
            o_ref[v, 0, r0:r0 + ROLL_ROWS, :] = jnp.where(kj >= PAD - (v - 1) * ATT_SUB, tile, NEG)


def _bias_table(rel_bias_padded):
    return pl.pallas_call(
        _bias_kernel,
        grid=(N_HEADS,),
        in_specs=[_const_spec((N_HEADS, 2 * LANES))],
        out_specs=pl.BlockSpec((N_VARIANTS, 1, ATT_WIN, ATT_SUB), lambda h: (0, h, 0, 0)),
        out_shape=jax.ShapeDtypeStruct((N_VARIANTS, N_HEADS, ATT_WIN, ATT_SUB), F32),
        scratch_shapes=[pltpu.VMEM((N_HEADS, BASE_W), F32)],
        compiler_params=pltpu.CompilerParams(
            dimension_semantics=("arbitrary",), vmem_limit_bytes=VMEM_LIMIT),
        name="rel_bias_table",
    )(rel_bias_padded)


def _attn_kernel(qt_ref, kp_ref, kc_ref, vtp_ref, vtc_ref, tbl_ref, o_ref, kcat, vtcat, st_s, p_s):
    i = pl.program_id(2)
    kcat[0:PAD, :] = kp_ref[0]
    kcat[PAD:, :] = kc_ref[0]
    for hh in range(2):
        src = slice(hh * HEAD_DIM, (hh + 1) * HEAD_DIM)
        r0 = hh * VT_ROWS
        vtcat[r0:r0 + HEAD_DIM, 0:PAD] = vtp_ref[0, src, :]
        vtcat[r0:r0 + HEAD_DIM, PAD:] = vtc_ref[0, src, :]
        vtcat[r0 + HEAD_DIM:r0 + VT_ROWS, :] = jnp.ones((VT_ROWS - HEAD_DIM, PAD + ATT_TILE), BF16)

    feat = lax.broadcasted_iota(jnp.int32, (LANES, ATT_SUB), 0)
    first_head = feat < HEAD_DIM
    n_blk = ATT_TILE // ATT_SUB
    units = [(blk, hh) for blk in range(n_blk) for hh in range(2)]

    def scores(u):
        blk, hh = units[u]
        c0 = blk * ATT_SUB
        qt = qt_ref[0, :, c0:c0 + ATT_SUB]
        kwin = kcat[c0:c0 + ATT_WIN, :]
        variant = jnp.where(i == 0, blk + 1, 0) if blk + 1 < N_VARIANTS else 0
        mask = first_head if hh == 0 else jnp.logical_not(first_head)
        qm = jnp.where(mask, qt, jnp.zeros_like(qt))
        st = _dot(kwin, qm) + tbl_ref[variant, hh]
        st_s[u % ATT_SLOTS] = st
        return jnp.max(st, axis=0, keepdims=True)

    def softmax_pv(u, m):
        blk, hh = units[u]
        c0 = blk * ATT_SUB
        p_s[u % ATT_SLOTS] = jnp.exp2((st_s[u % ATT_SLOTS] - m).astype(BF16))
        vt = vtcat[hh * VT_ROWS:(hh + 1) * VT_ROWS, c0:c0 + ATT_WIN]
        ot = _dot(vt, p_s[u % ATT_SLOTS])
        return ot[:HEAD_DIM] * (1.0 / ot[HEAD_DIM:HEAD_DIM + 1])

    outs = []
    ms = [scores(u) for u in range(ATT_LOOKAHEAD)]
    for u in range(len(units)):
        if u + ATT_LOOKAHEAD < len(units):
            ms.append(scores(u + ATT_LOOKAHEAD))
        outs.append(softmax_pv(u, ms[u]))
        if u % 2 == 1:
            c0 = units[u][0] * ATT_SUB
            pair = jnp.concatenate(outs[-2:], axis=0)
            o_ref[0, c0:c0 + ATT_SUB, :] = pair.T.astype(BF16)


def _attention(qt, k, vt, table):
    bsz, s, _ = k.shape
    t = ATT_TILE
    prev_idx = lambda i: jnp.maximum(i * (t // PAD) - 1, 0)
    return pl.pallas_call(
        _attn_kernel,
        grid=(HEAD_PAIRS, bsz, s // t),
        in_specs=[
            pl.BlockSpec((1, LANES, t), lambda p, b, i: (b, p, i)),
            pl.BlockSpec((1, PAD, LANES), lambda p, b, i: (b, prev_idx(i), p)),
            pl.BlockSpec((1, t, LANES), lambda p, b, i: (b, i, p)),
            pl.BlockSpec((1, LANES, PAD), lambda p, b, i: (b, p, prev_idx(i))),
            pl.BlockSpec((1, LANES, t), lambda p, b, i: (b, p, i)),
            pl.BlockSpec((N_VARIANTS, 2, ATT_WIN, ATT_SUB), lambda p, b, i: (0, p, 0, 0)),
        ],
        out_specs=pl.BlockSpec((1, t, LANES), lambda p, b, i: (b, i, p)),
        out_shape=jax.ShapeDtypeStruct(k.shape, BF16),
        scratch_shapes=[pltpu.VMEM((PAD + t, LANES), BF16), pltpu.VMEM((2 * VT_ROWS, PAD + t), BF16),
                        pltpu.VMEM((ATT_SLOTS, ATT_WIN, ATT_SUB), F32),
                        pltpu.VMEM((ATT_SLOTS, ATT_WIN, ATT_SUB), BF16)],
        compiler_params=pltpu.CompilerParams(
            dimension_semantics=("arbitrary", "arbitrary", "arbitrary"), vmem_limit_bytes=VMEM_LIMIT),
        name="band_attention",
    )(qt, k, k, vt, vt, table)


def kernel(x, a_norm, a_w_in, a_conv_w, a_conv_b, a_w_gate, a_b_gate, a_lambda, a_w_out, kv_norm, w_kv, k_norm, b_norm, b_w_q, b_q_norm, b_rel_bias, b_w_o, mlp_norm, w_up, w_down):
    bsz, s, d = x.shape
    assert d == D_MODEL and s % ATT_TILE == 0 and s % REC_TILE == 0
    assert a_norm.shape[0] == 1 and b_norm.shape[0] == 1 and mlp_norm.shape[0] == 2
    n = bsz * s
    row = lambda p: p.reshape(1, -1).astype(F32)

    h = _layer0(x.reshape(n, d), s // REC_TILE, row(a_norm[0]), a_w_in[0].astype(BF16), a_conv_w[0],
                row(a_conv_b[0]), a_w_gate[0].astype(BF16), a_b_gate[0], row(a_lambda[0]),
                a_w_out[0].astype(BF16), row(mlp_norm[0]), w_up[0].astype(BF16), w_down[0].astype(BF16))

    col = lambda g: jnp.broadcast_to(jnp.tile(g, N_HEADS).astype(F32)[:, None], (D_MODEL, LANES))
    q_gain = col(b_q_norm[0]) * (HEAD_DIM ** -0.5 * LOG2E)
    k_gain = col(k_norm)
    shp = (bsz, s, d)
    qt, k, vt = _kvq(h.reshape(shp), row(kv_norm), row(b_norm[0]), w_kv[:, :D_MODEL].T.astype(BF16),
                     w_kv[:, D_MODEL:].T.astype(BF16), b_w_q[0].T.astype(BF16), k_gain, q_gain)
    table = _bias_table(jnp.pad(b_rel_bias[0], ((0, 0), (0, 2 * LANES - NREL))))
    attn = _attention(qt, k, vt, table)
    out = _proj_mlp(attn.reshape(n, d), b_w_o[0].astype(BF16), h, row(mlp_norm[1]),
                    w_up[1].astype(BF16), w_down[1].astype(BF16))
    return out.reshape(shp)
```

```python
import functools

import jax
import jax.numpy as jnp
from jax import lax
from jax.experimental import pallas as pl
from jax.experimental.pallas import tpu as pltpu

D_MODEL = 1024
D_RNN = D_MODEL
LRU_BLOCKS = 8
LRU_BW = D_RNN // LRU_BLOCKS
CONV_W = 4
LRU_C = 8.0
N_HEADS = 16
HEAD_DIM = 64
CHUNK = 64
LEFT_CHUNKS = 8
PAD = LEFT_CHUNKS * CHUNK
MAX_REL = 2 * CHUNK
MIN_REL = -(CHUNK - 1)
NREL = MAX_REL - MIN_REL + 1
D_FF = 4 * D_MODEL
EPS = 1e-6

LANES = 128
SUBLANES = 8
TINY = 1.1754944e-38
NEG = -1e30

REC_TILE = 256
L0_FF_CHUNK = 1024
L0_ORDER = ("rec", "conv", "g0", "m", "g1", "gate", "m", "s0", "s1", "m", "s2", "s3", "gelu", "m", "out")
MLP_TILE = 1024
KVQ_TILE = 1024
KVQ_SUB = 256
ATT_TILE = 4096
ATT_SUB = 256
ATT_WIN = PAD + ATT_SUB
ATT_LOOKAHEAD = 4
ATT_SLOTS = ATT_LOOKAHEAD + 2
ATT_KEY_ROWS = 384
VT_ROWS = HEAD_DIM + 16
LOG2E = 1.4426950408889634
HEAD_PAIRS = N_HEADS // 2
VMEM_LIMIT = 56 * 1024 * 1024

F32 = jnp.float32
BF16 = jnp.bfloat16


def _dot(a, b):
    return jnp.dot(a, b, preferred_element_type=F32)


def _rms_scale(x):
    return lax.rsqrt(jnp.mean(x * x, axis=-1, keepdims=True) + EPS)


def _const_spec(shape):
    zeros = (0,) * len(shape)
    return pl.BlockSpec(shape, lambda *_: zeros, pipeline_mode=pl.Buffered(1))


def _layer0_kernel(x_ref, norm_ref, w_in_ref, cw_ref, cb_ref, wg_ref, bg_ref, lam_ref, w_out_ref,
                   mnorm_ref, w_up_ref, w_down_ref, o_ref,
                   rec_ext, a_s, b_s, h_carry, mix_s, *, tiles_per_seq):
    t = REC_TILE
    j = pl.program_id(0)

    @pl.when(j % tiles_per_seq == 0)
    def _():
        rec_ext[:, 0:SUBLANES, :] = jnp.zeros((D_RNN // LANES, SUBLANES, LANES), F32)
        h_carry[...] = jnp.zeros_like(h_carry)

    @pl.when(j == 0)
    def _():
        mix_s[1] = jnp.zeros((t, D_MODEL), F32)

    slot = j % 2

    hp = mix_s[1 - slot]
    hn = (hp * _rms_scale(hp) * mnorm_ref[...]).astype(BF16)

    mlp_state = {"acc": hp, "k": 0}

    def mlp_chunk():
        k = mlp_state["k"]
        sl = slice(k * L0_FF_CHUNK, (k + 1) * L0_FF_CHUNK)
        up = jnp.maximum(_dot(hn, w_up_ref[:, sl]), 0.0)
        mlp_state["acc"] = mlp_state["acc"] + _dot((up * up).astype(BF16), w_down_ref[sl, :])
        mlp_state["k"] = k + 1

    x = x_ref[...]
    xn = (x * _rms_scale(x) * norm_ref[...]).astype(BF16)
    v = {"h": h_carry[...]}
    row = lax.broadcasted_iota(jnp.int32, (SUBLANES, D_RNN), 0)
    n_slabs = t // SUBLANES

    def rec_proj():
        v["rec"] = _dot(xn, w_in_ref[:, D_RNN:])

    def gate_proj():
        v["gate"] = _dot(xn, w_in_ref[:, :D_RNN])

    def conv():
        cs = []
        for n in range(D_RNN // LANES):
            sl = slice(n * LANES, (n + 1) * LANES)
            rec = v["rec"][:, sl]
            rec_ext[n, SUBLANES:SUBLANES + t, :] = rec
            c = cb_ref[:, sl] + rec * cw_ref[CONV_W - 1:CONV_W, sl]
            for k in range(1, CONV_W):
                c = c + rec_ext[n, pl.ds(SUBLANES - k, t), :] * cw_ref[CONV_W - 1 - k:CONV_W - k, sl]
            cs.append(c)
            rec_ext[n, 0:SUBLANES, :] = rec_ext[n, t:t + SUBLANES, :]
        v["c"] = cs

    def gates(half):
        neg_lam = -lam_ref[...]
        softplus = jnp.maximum(neg_lam, 0.0) + jnp.log1p(jnp.exp(-jnp.abs(neg_lam)))
        for n in range(half * LRU_BLOCKS // 2, (half + 1) * LRU_BLOCKS // 2):
            sl = slice(n * LRU_BW, (n + 1) * LRU_BW)
            cb = v["c"][n]
            g = _dot(cb.astype(BF16), wg_ref[n]) + bg_ref[n:n + 1, :]
            r = jax.nn.sigmoid(g[:, :LRU_BW])
            i = jax.nn.sigmoid(g[:, LRU_BW:])
            log_a = (-LRU_C) * r * softplus[:, sl]
            a = jnp.exp(log_a)
            z = 1.0 - a * a
            mult = z * lax.rsqrt(jnp.maximum(z, TINY))
            a_s[:, sl] = a
            b_s[:, sl] = mult * (i * cb)

    def scan(quarter):
        for s in range(quarter * n_slabs // 4, (quarter + 1) * n_slabs // 4):
            rows = slice(s * SUBLANES, (s + 1) * SUBLANES)
            a = a_s[rows, :]
            b = b_s[rows, :]
            for d in (1, 2, 4):
                keep = row >= d
                a_sh = jnp.where(keep, pltpu.roll(a, d, 0), 1.0)
                b_sh = jnp.where(keep, pltpu.roll(b, d, 0), 0.0)
                b = a * b_sh + b
                a = a * a_sh
            h = a * v["h"] + b
            b_s[rows, :] = h
            v["h"] = h[SUBLANES - 1:SUBLANES, :]

    def gelu_gate():
        v["y"] = (jax.nn.gelu(v["gate"]) * b_s[...]).astype(BF16)

    def out_proj():
        mix_s[slot] = x + _dot(v["y"], w_out_ref[...])

    phases = {
        "rec": rec_proj, "gate": gate_proj, "conv": conv, "g0": lambda: gates(0), "g1": lambda: gates(1),
        "s0": lambda: scan(0), "s1": lambda: scan(1), "s2": lambda: scan(2), "s3": lambda: scan(3),
        "gelu": gelu_gate, "out": out_proj, "m": mlp_chunk,
    }
    for name in L0_ORDER:
        phases[name]()
    assert mlp_state["k"] == D_FF // L0_FF_CHUNK
    h_carry[...] = v["h"]
    o_ref[...] = mlp_state["acc"]


def _layer0(x2d, tiles_per_seq, norm, w_in, conv_w, conv_b, w_gate, b_gate, lam, w_out, mlp_norm, w_up, w_down):
    n_tiles = x2d.shape[0] // REC_TILE
    t = REC_TILE
    return pl.pallas_call(
        functools.partial(_layer0_kernel, tiles_per_seq=tiles_per_seq),
        grid=(n_tiles + 1,),
        in_specs=[
            pl.BlockSpec((t, D_MODEL), lambda j: (jnp.minimum(j, n_tiles - 1), 0)),
            _const_spec((1, D_MODEL)),
            _const_spec((D_MODEL, 2 * D_RNN)),
            _const_spec((CONV_W, D_RNN)),
            _const_spec((1, D_RNN)),
            _const_spec((LRU_BLOCKS, LRU_BW, 2 * LRU_BW)),
            _const_spec((LRU_BLOCKS, 2 * LRU_BW)),
            _const_spec((1, D_RNN)),
            _const_spec((D_RNN, D_MODEL)),
            _const_spec((1, D_MODEL)),
            _const_spec((D_MODEL, D_FF)),
            _const_spec((D_FF, D_MODEL)),
        ],
        out_specs=pl.BlockSpec((t, D_MODEL), lambda j: (jnp.maximum(j - 1, 0), 0)),
        out_shape=jax.ShapeDtypeStruct(x2d.shape, F32),
        scratch_shapes=[
            pltpu.VMEM((D_RNN // LANES, t + SUBLANES, LANES), F32),
            pltpu.VMEM((t, D_RNN), F32),
            pltpu.VMEM((t, D_RNN), F32),
            pltpu.VMEM((1, D_RNN), F32),
            pltpu.VMEM((2, t, D_MODEL), F32),
        ],
        compiler_params=pltpu.CompilerParams(
            dimension_semantics=("arbitrary",), vmem_limit_bytes=VMEM_LIMIT),
        name="layer0_mixer_mlp",
    )(x2d, norm, w_in, conv_w, conv_b, w_gate, b_gate, lam, w_out, mlp_norm, w_up, w_down)


FF_CHUNK = 1024


def _mlp_body(h, norm_ref, w_up_ref, w_down_ref):
    hn = (h * _rms_scale(h) * norm_ref[...]).astype(BF16)
    acc = h
    for j in range(D_FF // FF_CHUNK):
        sl = slice(j * FF_CHUNK, (j + 1) * FF_CHUNK)
        up = jnp.maximum(_dot(hn, w_up_ref[:, sl]), 0.0)
        acc = acc + _dot((up * up).astype(BF16), w_down_ref[sl, :])
    return acc


def _proj_mlp_kernel(a_ref, w_o_ref, x_ref, norm_ref, w_up_ref, w_down_ref, o_ref):
    h = x_ref[...] + _dot(a_ref[...], w_o_ref[...])
    o_ref[...] = _mlp_body(h, norm_ref, w_up_ref, w_down_ref)


def _proj_mlp(attn, w_o, x2d, norm, w_up, w_down):
    n = x2d.shape[0]
    t = MLP_TILE
    tile = pl.BlockSpec((t, D_MODEL), lambda i: (i, 0))
    return pl.pallas_call(
        _proj_mlp_kernel,
        grid=(n // t,),
        in_specs=[tile, _const_spec((D_MODEL, D_MODEL)), tile, _const_spec((1, D_MODEL)),
                  _const_spec((D_MODEL, D_FF)), _const_spec((D_FF, D_MODEL))],
        out_specs=tile,
        out_shape=jax.ShapeDtypeStruct(x2d.shape, F32),
        compiler_params=pltpu.CompilerParams(
            dimension_semantics=("arbitrary",), vmem_limit_bytes=VMEM_LIMIT),
        name="proj_mlp",
    )(attn, w_o, x2d, norm, w_up, w_down)


def _dot_nt(a, b):
    return lax.dot_general(a, b, (((1,), (1,)), ((), ())), preferred_element_type=F32)


def _head_rms_normalise_t(zt, gain_ref):
    t = zt.shape[1]
    z3 = zt.reshape(N_HEADS, HEAD_DIM, t)
    inv = lax.rsqrt(jnp.mean(z3 * z3, axis=1, keepdims=True) + EPS)
    gain = jnp.concatenate([gain_ref[...]] * (t // LANES), axis=1)
    return (z3 * inv).reshape(N_HEADS * HEAD_DIM, t) * gain


def _kvq_kernel(x_ref, kvn_ref, qn_ref, w_kt_ref, w_vt_ref, w_qt_ref, kg_ref, qg_ref,
                qt_ref, k_ref, vt_ref):
    def project(r0):
        x = x_ref[0, r0:r0 + KVQ_SUB, :]
        xs = x * _rms_scale(x)
        xkv = (xs * kvn_ref[...]).astype(BF16)
        xq = (xs * qn_ref[...]).astype(BF16)
        return _dot_nt(w_vt_ref[...], xkv), _dot_nt(w_kt_ref[...], xkv), _dot_nt(w_qt_ref[...], xq)

    def finish(r0, vt, kt, qt):
        rows = slice(r0, r0 + KVQ_SUB)
        vt_ref[0, :, rows] = vt.astype(BF16)
        k_ref[0, rows, :] = _head_rms_normalise_t(kt, kg_ref).T.astype(BF16)
        qt_ref[0, :, rows] = _head_rms_normalise_t(qt, qg_ref).astype(BF16)

    starts = list(range(0, KVQ_TILE, KVQ_SUB))
    pending = project(starts[0])
    for n, r0 in enumerate(starts):
        nxt = project(starts[n + 1]) if n + 1 < len(starts) else None
        finish(r0, *pending)
        pending = nxt


def _kvq(x, kv_norm, q_norm, w_kt, w_vt, w_qt, k_gain, q_gain):
    bsz, s, _ = x.shape
    t = KVQ_TILE
    tok_major = pl.BlockSpec((1, t, D_MODEL), lambda b, i: (b, i, 0))
    feat_major = pl.BlockSpec((1, D_MODEL, t), lambda b, i: (b, 0, i))
    w_spec = _const_spec((D_MODEL, D_MODEL))
    g_spec = _const_spec((D_MODEL, LANES))
    ft = jax.ShapeDtypeStruct((bsz, D_MODEL, s), BF16)
    return pl.pallas_call(
        _kvq_kernel,
        grid=(bsz, s // t),
        in_specs=[tok_major, _const_spec((1, D_MODEL)), _const_spec((1, D_MODEL)),
                  w_spec, w_spec, w_spec, g_spec, g_spec],
        out_specs=[feat_major, tok_major, feat_major],
        out_shape=[ft, jax.ShapeDtypeStruct(x.shape, BF16), ft],
        compiler_params=pltpu.CompilerParams(
            dimension_semantics=("arbitrary", "arbitrary"), vmem_limit_bytes=VMEM_LIMIT),
        name="kvq_proj",
    )(x, kv_norm, q_norm, w_kt, w_vt, w_qt, k_gain, q_gain)


BASE_W = ATT_WIN + ATT_SUB
N_VARIANTS = 1 + PAD // ATT_SUB
ROLL_ROWS = 128


def _bias_kernel(rb_ref, o_ref, base_s):
    rb = rb_ref[...] * LOG2E
    r_idx = lax.broadcasted_iota(jnp.int32, (2 * LANES, BASE_W), 0)
    m_idx = lax.broadcasted_iota(jnp.int32, (2 * LANES, BASE_W), 1)
    want = jnp.clip(m_idx - ATT_SUB, MIN_REL, MAX_REL) - MIN_REL
    onehot = (r_idx == want).astype(BF16)
    p0 = rb.astype(BF16)
    r1 = rb - p0.astype(F32)
    p1 = r1.astype(BF16)
    p2 = (r1 - p1.astype(F32)).astype(BF16)
    base_s[...] = _dot(p0, onehot) + _dot(p1, onehot) + _dot(p2, onehot)
    base = base_s[pl.ds(pl.program_id(0), 1), :]

    for r0 in range(0, ATT_WIN, ROLL_ROWS):
        kj = r0 + lax.broadcasted_iota(jnp.int32, (ROLL_ROWS, ATT_SUB), 0)
        qi = lax.broadcasted_iota(jnp.int32, (ROLL_ROWS, ATT_SUB), 1)
        band_lo = (qi // CHUNK) * CHUNK
        in_band = (kj >= band_lo) & (kj < band_lo + PAD + CHUNK)
        rows = jnp.broadcast_to(base, (ROLL_ROWS, BASE_W))
        toeplitz = pltpu.roll(rows, r0, 1, stride=1, stride_axis=0)
        tile = jnp.where(in_band, toeplitz[:, ATT_WIN:], NEG)
        o_ref[0, 0, r0:r0 + ROLL_ROWS, :] = tile
        for v in range(1, N_VARIANTS):
            o_ref[v, 0, r0:r0 + ROLL_ROWS, :] = jnp.where(kj >= PAD - (v - 1) * ATT_SUB, tile, NEG)


def _bias_table(rel_bias_padded):
    return pl.pallas_call(
        _bias_kernel,
        grid=(N_HEADS,),
        in_specs=[_const_spec((N_HEADS, 2 * LANES))],
        out_specs=pl.BlockSpec((N_VARIANTS, 1, ATT_WIN, ATT_SUB), lambda h: (0, h, 0, 0)),
        out_shape=jax.ShapeDtypeStruct((N_VARIANTS, N_HEADS, ATT_WIN, ATT_SUB), F32),
        scratch_shapes=[pltpu.VMEM((N_HEADS, BASE_W), F32)],
        compiler_params=pltpu.CompilerParams(
            dimension_semantics=("arbitrary",), vmem_limit_bytes=VMEM_LIMIT),
        name="rel_bias_table",
    )(rel_bias_padded)


def _attn_kernel(qt_ref, kp_ref, kc_ref, vtp_ref, vtc_ref, tbl_ref, o_ref, kcat, vtcat, st_s, p_s):
    i = pl.program_id(2)
    kcat[0:PAD, :] = kp_ref[0]
    kcat[PAD:, :] = kc_ref[0]
    for hh in range(2):
        src = slice(hh * HEAD_DIM, (hh + 1) * HEAD_DIM)
        r0 = hh * VT_ROWS
        vtcat[r0:r0 + HEAD_DIM, 0:PAD] = vtp_ref[0, src, :]
        vtcat[r0:r0 + HEAD_DIM, PAD:] = vtc_ref[0, src, :]
        vtcat[r0 + HEAD_DIM:r0 + VT_ROWS, :] = jnp.ones((VT_ROWS - HEAD_DIM, PAD + ATT_TILE), BF16)

    feat = lax.broadcasted_iota(jnp.int32, (LANES, ATT_SUB), 0)
    first_head = feat < HEAD_DIM
    n_blk = ATT_TILE // ATT_SUB
    units = [(blk, hh) for blk in range(n_blk) for hh in range(2)]

    def scores(u):
        blk, hh = units[u]
        c0 = blk * ATT_SUB
        qt = qt_ref[0, :, c0:c0 + ATT_SUB]
        kwin = kcat[c0:c0 + ATT_WIN, :]
        variant = jnp.where(i == 0, blk + 1, 0) if blk + 1 < N_VARIANTS else 0
        mask = first_head if hh == 0 else jnp.logical_not(first_head)
        qm = jnp.where(mask, qt, jnp.zeros_like(qt))
        m = None
        for r0 in range(0, ATT_WIN, ATT_KEY_ROWS):
            rows = slice(r0, r0 + ATT_KEY_ROWS)
            st = _dot(kwin[rows], qm) + tbl_ref[variant, hh, rows, :]
            st_s[u % ATT_SLOTS, rows, :] = st
            piece_max = jnp.max(st, axis=0, keepdims=True)
            m = piece_max if m is None else jnp.maximum(m, piece_max)
        return m

    def softmax_pv(u, m):
        blk, hh = units[u]
        c0 = blk * ATT_SUB
        p_s[u % ATT_SLOTS] = jnp.exp2((st_s[u % ATT_SLOTS] - m).astype(BF16))
        vt = vtcat[hh * VT_ROWS:(hh + 1) * VT_ROWS, c0:c0 + ATT_WIN]
        ot = _dot(vt, p_s[u % ATT_SLOTS])
        return ot[:HEAD_DIM] * (1.0 / ot[HEAD_DIM:HEAD_DIM + 1])

    outs = []
    ms = [scores(u) for u in range(ATT_LOOKAHEAD)]
    for u in range(len(units)):
        if u + ATT_LOOKAHEAD < len(units):
            ms.append(scores(u + ATT_LOOKAHEAD))
        outs.append(softmax_pv(u, ms[u]))
        if u % 2 == 1:
            c0 = units[u][0] * ATT_SUB
            pair = jnp.concatenate(outs[-2:], axis=0)
            o_ref[0, c0:c0 + ATT_SUB, :] = pair.T.astype(BF16)


def _attention(qt, k, vt, table):
    bsz, s, _ = k.shape
    t = ATT_TILE
    prev_idx = lambda i: jnp.maximum(i * (t // PAD) - 1, 0)
    return pl.pallas_call(
        _attn_kernel,
        grid=(HEAD_PAIRS, bsz, s // t),
        in_specs=[
            pl.BlockSpec((1, LANES, t), lambda p, b, i: (b, p, i)),
            pl.BlockSpec((1, PAD, LANES), lambda p, b, i: (b, prev_idx(i), p)),
            pl.BlockSpec((1, t, LANES), lambda p, b, i: (b, i, p)),
            pl.BlockSpec((1, LANES, PAD), lambda p, b, i: (b, p, prev_idx(i))),
            pl.BlockSpec((1, LANES, t), lambda p, b, i: (b, p, i)),
            pl.BlockSpec((N_VARIANTS, 2, ATT_WIN, ATT_SUB), lambda p, b, i: (0, p, 0, 0)),
        ],
        out_specs=pl.BlockSpec((1, t, LANES), lambda p, b, i: (b, i, p)),
        out_shape=jax.ShapeDtypeStruct(k.shape, BF16),
        scratch_shapes=[pltpu.VMEM((PAD + t, LANES), BF16), pltpu.VMEM((2 * VT_ROWS, PAD + t), BF16),
                        pltpu.VMEM((ATT_SLOTS, ATT_WIN, ATT_SUB), F32),
                        pltpu.VMEM((ATT_SLOTS, ATT_WIN, ATT_SUB), BF16)],
        compiler_params=pltpu.CompilerParams(
            dimension_semantics=("arbitrary", "arbitrary", "arbitrary"), vmem_limit_bytes=VMEM_LIMIT),
        name="band_attention",
    )(qt, k, k, vt, vt, table)


def kernel(x, a_norm, a_w_in, a_conv_w, a_conv_b, a_w_gate, a_b_gate, a_lambda, a_w_out, kv_norm, w_kv, k_norm, b_norm, b_w_q, b_q_norm, b_rel_bias, b_w_o, mlp_norm, w_up, w_down):
    bsz, s, d = x.shape
    assert d == D_MODEL and s % ATT_TILE == 0 and s % REC_TILE == 0
    assert a_norm.shape[0] == 1 and b_norm.shape[0] == 1 and mlp_norm.shape[0] == 2
    n = bsz * s
    row = lambda p: p.reshape(1, -1).astype(F32)

    h = _layer0(x.reshape(n, d), s // REC_TILE, row(a_norm[0]), a_w_in[0].astype(BF16), a_conv_w[0],
                row(a_conv_b[0]), a_w_gate[0].astype(BF16), a_b_gate[0], row(a_lambda[0]),
                a_w_out[0].astype(BF16), row(mlp_norm[0]), w_up[0].astype(BF16), w_down[0].astype(BF16))

    col = lambda g: jnp.broadcast_to(jnp.tile(g, N_HEADS).astype(F32)[:, None], (D_MODEL, LANES))
    q_gain = col(b_q_norm[0]) * (HEAD_DIM ** -0.5 * LOG2E)
    k_gain = col(k_norm)
    shp = (bsz, s, d)
    qt, k, vt = _kvq(h.reshape(shp), row(kv_norm), row(b_norm[0]), w_kv[:, :D_MODEL].T.astype(BF16),
                     w_kv[:, D_MODEL:].T.astype(BF16), b_w_q[0].T.astype(BF16), k_gain, q_gain)
    table = _bias_table(jnp.pad(b_rel_bias[0], ((0, 0), (0, 2 * LANES - NREL))))
    attn = _attention(qt, k, vt, table)
    out = _proj_mlp(attn.reshape(n, d), b_w_o[0].astype(BF16), h, row(mlp_norm[1]),
                    w_up[1].astype(BF16), w_down[1].astype(BF16))
    return out.reshape(shp)
```

```python
import functools

import jax
import jax.numpy as jnp
from jax import lax
from jax.experimental import pallas as pl
from jax.experimental.pallas import tpu as pltpu

D_MODEL = 1024
D_RNN = D_MODEL
LRU_BLOCKS = 8
LRU_BW = D_RNN // LRU_BLOCKS
CONV_W = 4
LRU_C = 8.0
N_HEADS = 16
HEAD_DIM = 64
CHUNK = 64
LEFT_CHUNKS = 8
PAD = LEFT_CHUNKS * CHUNK
MAX_REL = 2 * CHUNK
MIN_REL = -(CHUNK - 1)
NREL = MAX_REL - MIN_REL + 1
D_FF = 4 * D_MODEL
EPS = 1e-6

LANES = 128
SUBLANES = 8
TINY = 1.1754944e-38
NEG = -1e30

REC_TILE = 256
L0_SUBS = 2
L0_FF_CHUNK = 1024
L0_ORDER = ("rec", "conv", "g0", "m", "g1", "gate", "m", "s0", "s1", "m", "s2", "s3", "gelu", "m", "out")
MLP_TILE = 1024
KVQ_TILE = 1024
KVQ_SUB = 256
ATT_TILE = 4096
ATT_SUB = 256
ATT_WIN = PAD + ATT_SUB
ATT_LOOKAHEAD = 4
ATT_SLOTS = ATT_LOOKAHEAD + 2
ATT_KEY_ROWS = 384
VT_ROWS = HEAD_DIM + 16
LOG2E = 1.4426950408889634
HEAD_PAIRS = N_HEADS // 2
VMEM_LIMIT = 56 * 1024 * 1024

F32 = jnp.float32
BF16 = jnp.bfloat16


def _dot(a, b):
    return jnp.dot(a, b, preferred_element_type=F32)


def _rms_scale(x):
    return lax.rsqrt(jnp.mean(x * x, axis=-1, keepdims=True) + EPS)


def _const_spec(shape):
    zeros = (0,) * len(shape)
    return pl.BlockSpec(shape, lambda *_: zeros, pipeline_mode=pl.Buffered(1))


def _layer0_kernel(x_ref, norm_ref, w_in_ref, cw_ref, cb_ref, wg_ref, bg_ref, lam_ref, w_out_ref,
                   mnorm_ref, w_up_ref, w_down_ref, o_ref,
                   rec_ext, a_s, b_s, h_carry, mix_s, *, tiles_per_seq):
    t = REC_TILE
    j = pl.program_id(0)

    @pl.when(j % tiles_per_seq == 0)
    def _():
        rec_ext[:, 0:SUBLANES, :] = jnp.zeros((D_RNN // LANES, SUBLANES, LANES), F32)
        h_carry[...] = jnp.zeros_like(h_carry)

    @pl.when(j == 0)
    def _():
        mix_s[1] = jnp.zeros((L0_SUBS, t, D_MODEL), F32)

    slot = j % 2
    row = lax.broadcasted_iota(jnp.int32, (SUBLANES, D_RNN), 0)
    n_slabs = t // SUBLANES
    h_state = h_carry[...]
    for q in range(L0_SUBS):
        h_state = _layer0_sub_tile(
            q, slot, h_state, row, n_slabs, x_ref, norm_ref, w_in_ref, cw_ref, cb_ref, wg_ref, bg_ref,
            lam_ref, w_out_ref, mnorm_ref, w_up_ref, w_down_ref, o_ref, rec_ext, a_s, b_s, mix_s)
    h_carry[...] = h_state


def _layer0_sub_tile(q, slot, h_in, row, n_slabs, x_ref, norm_ref, w_in_ref, cw_ref, cb_ref, wg_ref,
                     bg_ref, lam_ref, w_out_ref, mnorm_ref, w_up_ref, w_down_ref, o_ref,
                     rec_ext, a_s, b_s, mix_s):
    t = REC_TILE
    rows_q = slice(q * t, (q + 1) * t)

    hp = mix_s[1 - slot, q]
    hn = (hp * _rms_scale(hp) * mnorm_ref[...]).astype(BF16)

    mlp_state = {"acc": hp, "k": 0}

    def mlp_chunk():
        k = mlp_state["k"]
        sl = slice(k * L0_FF_CHUNK, (k + 1) * L0_FF_CHUNK)
        up = jnp.maximum(_dot(hn, w_up_ref[:, sl]), 0.0)
        mlp_state["acc"] = mlp_state["acc"] + _dot((up * up).astype(BF16), w_down_ref[sl, :])
        mlp_state["k"] = k + 1

    x = x_ref[rows_q, :]
    xn = (x * _rms_scale(x) * norm_ref[...]).astype(BF16)
    v = {"h": h_in}

    def rec_proj():
        v["rec"] = _dot(xn, w_in_ref[:, D_RNN:])

    def gate_proj():
        v["gate"] = _dot(xn, w_in_ref[:, :D_RNN])

    def conv():
        cs = []
        for n in range(D_RNN // LANES):
            sl = slice(n * LANES, (n + 1) * LANES)
            rec = v["rec"][:, sl]
            rec_ext[n, SUBLANES:SUBLANES + t, :] = rec
            c = cb_ref[:, sl] + rec * cw_ref[CONV_W - 1:CONV_W, sl]
            for k in range(1, CONV_W):
                c = c + rec_ext[n, pl.ds(SUBLANES - k, t), :] * cw_ref[CONV_W - 1 - k:CONV_W - k, sl]
            cs.append(c)
            rec_ext[n, 0:SUBLANES, :] = rec_ext[n, t:t + SUBLANES, :]
        v["c"] = cs

    def gates(half):
        neg_lam = -lam_ref[...]
        softplus = jnp.maximum(neg_lam, 0.0) + jnp.log1p(jnp.exp(-jnp.abs(neg_lam)))
        for n in range(half * LRU_BLOCKS // 2, (half + 1) * LRU_BLOCKS // 2):
            sl = slice(n * LRU_BW, (n + 1) * LRU_BW)
            cb = v["c"][n]
            g = _dot(cb.astype(BF16), wg_ref[n]) + bg_ref[n:n + 1, :]
            r = jax.nn.sigmoid(g[:, :LRU_BW])
            i = jax.nn.sigmoid(g[:, LRU_BW:])
            log_a = (-LRU_C) * r * softplus[:, sl]
            a = jnp.exp(log_a)
            z = 1.0 - a * a
            mult = z * lax.rsqrt(jnp.maximum(z, TINY))
            a_s[q, :, sl] = a
            b_s[q, :, sl] = mult * (i * cb)

    def scan(quarter):
        for s in range(quarter * n_slabs // 4, (quarter + 1) * n_slabs // 4):
            rows = slice(s * SUBLANES, (s + 1) * SUBLANES)
            a = a_s[q, rows, :]
            b = b_s[q, rows, :]
            for d in (1, 2, 4):
                keep = row >= d
                a_sh = jnp.where(keep, pltpu.roll(a, d, 0), 1.0)
                b_sh = jnp.where(keep, pltpu.roll(b, d, 0), 0.0)
                b = a * b_sh + b
                a = a * a_sh
            h = a * v["h"] + b
            b_s[q, rows, :] = h
            v["h"] = h[SUBLANES - 1:SUBLANES, :]

    def gelu_gate():
        v["y"] = (jax.nn.gelu(v["gate"]) * b_s[q]).astype(BF16)

    def out_proj():
        mix_s[slot, q] = x + _dot(v["y"], w_out_ref[...])

    phases = {
        "rec": rec_proj, "gate": gate_proj, "conv": conv, "g0": lambda: gates(0), "g1": lambda: gates(1),
        "s0": lambda: scan(0), "s1": lambda: scan(1), "s2": lambda: scan(2), "s3": lambda: scan(3),
        "gelu": gelu_gate, "out": out_proj, "m": mlp_chunk,
    }
    for name in L0_ORDER:
        phases[name]()
    assert mlp_state["k"] == D_FF // L0_FF_CHUNK
    o_ref[rows_q, :] = mlp_state["acc"]
    return v["h"]


def _layer0(x2d, tiles_per_seq, norm, w_in, conv_w, conv_b, w_gate, b_gate, lam, w_out, mlp_norm, w_up, w_down):
    t = REC_TILE
    blk = L0_SUBS * t
    n_tiles = x2d.shape[0] // blk
    return pl.pallas_call(
        functools.partial(_layer0_kernel, tiles_per_seq=tiles_per_seq),
        grid=(n_tiles + 1,),
        in_specs=[
            pl.BlockSpec((blk, D_MODEL), lambda j: (jnp.minimum(j, n_tiles - 1), 0)),
            _const_spec((1, D_MODEL)),
            _const_spec((D_MODEL, 2 * D_RNN)),
            _const_spec((CONV_W, D_RNN)),
            _const_spec((1, D_RNN)),
            _const_spec((LRU_BLOCKS, LRU_BW, 2 * LRU_BW)),
            _const_spec((LRU_BLOCKS, 2 * LRU_BW)),
            _const_spec((1, D_RNN)),
            _const_spec((D_RNN, D_MODEL)),
            _const_spec((1, D_MODEL)),
            _const_spec((D_MODEL, D_FF)),
            _const_spec((D_FF, D_MODEL)),
        ],
        out_specs=pl.BlockSpec((blk, D_MODEL), lambda j: (jnp.maximum(j - 1, 0), 0)),
        out_shape=jax.ShapeDtypeStruct(x2d.shape, F32),
        scratch_shapes=[
            pltpu.VMEM((D_RNN // LANES, t + SUBLANES, LANES), F32),
            pltpu.VMEM((L0_SUBS, t, D_RNN), F32),
            pltpu.VMEM((L0_SUBS, t, D_RNN), F32),
            pltpu.VMEM((1, D_RNN), F32),
            pltpu.VMEM((2, L0_SUBS, t, D_MODEL), F32),
        ],
        compiler_params=pltpu.CompilerParams(
            dimension_semantics=("arbitrary",), vmem_limit_bytes=VMEM_LIMIT),
        name="layer0_mixer_mlp",
    )(x2d, norm, w_in, conv_w, conv_b, w_gate, b_gate, lam, w_out, mlp_norm, w_up, w_down)


FF_CHUNK = 1024


def _mlp_body(h, norm_ref, w_up_ref, w_down_ref):
    hn = (h * _rms_scale(h) * norm_ref[...]).astype(BF16)
    acc = h
    for j in range(D_FF // FF_CHUNK):
        sl = slice(j * FF_CHUNK, (j + 1) * FF_CHUNK)
        up = jnp.maximum(_dot(hn, w_up_ref[:, sl]), 0.0)
        acc = acc + _dot((up * up).astype(BF16), w_down_ref[sl, :])
    return acc


def _proj_mlp_kernel(a_ref, w_o_ref, x_ref, norm_ref, w_up_ref, w_down_ref, o_ref):
    h = x_ref[...] + _dot(a_ref[...], w_o_ref[...])
    o_ref[...] = _mlp_body(h, norm_ref, w_up_ref, w_down_ref)


def _proj_mlp(attn, w_o, x2d, norm, w_up, w_down):
    n = x2d.shape[0]
    t = MLP_TILE
    tile = pl.BlockSpec((t, D_MODEL), lambda i: (i, 0))
    return pl.pallas_call(
        _proj_mlp_kernel,
        grid=(n // t,),
        in_specs=[tile, _const_spec((D_MODEL, D_MODEL)), tile, _const_spec((1, D_MODEL)),
                  _const_spec((D_MODEL, D_FF)), _const_spec((D_FF, D_MODEL))],
        out_specs=tile,
        out_shape=jax.ShapeDtypeStruct(x2d.shape, F32),
        compiler_params=pltpu.CompilerParams(
            dimension_semantics=("arbitrary",), vmem_limit_bytes=VMEM_LIMIT),
        name="proj_mlp",
    )(attn, w_o, x2d, norm, w_up, w_down)


def _dot_nt(a, b):
    return lax.dot_general(a, b, (((1,), (1,)), ((), ())), preferred_element_type=F32)


def _head_rms_normalise_t(zt, gain_ref):
    t = zt.shape[1]
    z3 = zt.reshape(N_HEADS, HEAD_DIM, t)
    inv = lax.rsqrt(jnp.mean(z3 * z3, axis=1, keepdims=True) + EPS)
    gain = jnp.concatenate([gain_ref[...]] * (t // LANES), axis=1)
    return (z3 * inv).reshape(N_HEADS * HEAD_DIM, t) * gain


def _kvq_kernel(x_ref, kvn_ref, qn_ref, w_kt_ref, w_vt_ref, w_qt_ref, kg_ref, qg_ref,
                qt_ref, k_ref, vt_ref):
    def project(r0):
        x = x_ref[0, r0:r0 + KVQ_SUB, :]
        xs = x * _rms_scale(x)
        xkv = (xs * kvn_ref[...]).astype(BF16)
        xq = (xs * qn_ref[...]).astype(BF16)
        return _dot_nt(w_vt_ref[...], xkv), _dot_nt(w_kt_ref[...], xkv), _dot_nt(w_qt_ref[...], xq)

    def finish(r0, vt, kt, qt):
        rows = slice(r0, r0 + KVQ_SUB)
        vt_ref[0, :, rows] = vt.astype(BF16)
        k_ref[0, rows, :] = _head_rms_normalise_t(kt, kg_ref).T.astype(BF16)
        qt_ref[0, :, rows] = _head_rms_normalise_t(qt, qg_ref).astype(BF16)

    starts = list(range(0, KVQ_TILE, KVQ_SUB))
    pending = project(starts[0])
    for n, r0 in enumerate(starts):
        nxt = project(starts[n + 1]) if n + 1 < len(starts) else None
        finish(r0, *pending)
        pending = nxt


def _kvq(x, kv_norm, q_norm, w_kt, w_vt, w_qt, k_gain, q_gain):
    bsz, s, _ = x.shape
    t = KVQ_TILE
    tok_major = pl.BlockSpec((1, t, D_MODEL), lambda b, i: (b, i, 0))
    feat_major = pl.BlockSpec((1, D_MODEL, t), lambda b, i: (b, 0, i))
    w_spec = _const_spec((D_MODEL, D_MODEL))
    g_spec = _const_spec((D_MODEL, LANES))
    ft = jax.ShapeDtypeStruct((bsz, D_MODEL, s), BF16)
    return pl.pallas_call(
        _kvq_kernel,
        grid=(bsz, s // t),
        in_specs=[tok_major, _const_spec((1, D_MODEL)), _const_spec((1, D_MODEL)),
                  w_spec, w_spec, w_spec, g_spec, g_spec],
        out_specs=[feat_major, tok_major, feat_major],
        out_shape=[ft, jax.ShapeDtypeStruct(x.shape, BF16), ft],
        compiler_params=pltpu.CompilerParams(
            dimension_semantics=("arbitrary", "arbitrary"), vmem_limit_bytes=VMEM_LIMIT),
        name="kvq_proj",
    )(x, kv_norm, q_norm, w_kt, w_vt, w_qt, k_gain, q_gain)


BASE_W = ATT_WIN + ATT_SUB
ROLL_ROWS = 128


def _bias_kernel(rb_ref, o_ref, base_s):
    rb = rb_ref[...] * LOG2E
    r_idx = lax.broadcasted_iota(jnp.int32, (2 * LANES, BASE_W), 0)
    m_idx = lax.broadcasted_iota(jnp.int32, (2 * LANES, BASE_W), 1)
    want = jnp.clip(m_idx - ATT_SUB, MIN_REL, MAX_REL) - MIN_REL
    onehot = (r_idx == want).astype(BF16)
    p0 = rb.astype(BF16)
    r1 = rb - p0.astype(F32)
    p1 = r1.astype(BF16)
    p2 = (r1 - p1.astype(F32)).astype(BF16)
    base_s[...] = _dot(p0, onehot) + _dot(p1, onehot) + _dot(p2, onehot)
    base = base_s[pl.ds(pl.program_id(0), 1), :]

    for r0 in range(0, ATT_WIN, ROLL_ROWS):
        kj = r0 + lax.broadcasted_iota(jnp.int32, (ROLL_ROWS, ATT_SUB), 0)
        qi = lax.broadcasted_iota(jnp.int32, (ROLL_ROWS, ATT_SUB), 1)
        band_lo = (qi // CHUNK) * CHUNK
        in_band = (kj >= band_lo) & (kj < band_lo + PAD + CHUNK)
        rows = jnp.broadcast_to(base, (ROLL_ROWS, BASE_W))
        toeplitz = pltpu.roll(rows, r0, 1, stride=1, stride_axis=0)
        o_ref[0, r0:r0 + ROLL_ROWS, :] = jnp.where(in_band, toeplitz[:, ATT_WIN:], NEG)


def _bias_table(rel_bias_padded):
    return pl.pallas_call(
        _bias_kernel,
        grid=(N_HEADS,),
        in_specs=[_const_spec((N_HEADS, 2 * LANES))],
        out_specs=pl.BlockSpec((1, ATT_WIN, ATT_SUB), lambda h: (h, 0, 0)),
        out_shape=jax.ShapeDtypeStruct((N_HEADS, ATT_WIN, ATT_SUB), F32),
        scratch_shapes=[pltpu.VMEM((N_HEADS, BASE_W), F32)],
        compiler_params=pltpu.CompilerParams(
            dimension_semantics=("arbitrary",), vmem_limit_bytes=VMEM_LIMIT),
        name="rel_bias_table",
    )(rel_bias_padded)


def _attn_kernel(qt_ref, kp_ref, kc_ref, vtp_ref, vtc_ref, tbl_ref, o_ref, kcat, vtcat, st_s, p_s):
    i = pl.program_id(2)
    kcat[0:PAD, :] = kp_ref[0]
    kcat[PAD:, :] = kc_ref[0]
    for hh in range(2):
        src = slice(hh * HEAD_DIM, (hh + 1) * HEAD_DIM)
        r0 = hh * VT_ROWS
        vtcat[r0:r0 + HEAD_DIM, 0:PAD] = vtp_ref[0, src, :]
        vtcat[r0:r0 + HEAD_DIM, PAD:] = vtc_ref[0, src, :]
        vtcat[r0 + HEAD_DIM:r0 + VT_ROWS, :] = jnp.ones((VT_ROWS - HEAD_DIM, PAD + ATT_TILE), BF16)

    feat = lax.broadcasted_iota(jnp.int32, (LANES, ATT_SUB), 0)
    first_head = feat < HEAD_DIM
    n_blk = ATT_TILE // ATT_SUB
    units = [(blk, hh) for blk in range(n_blk) for hh in range(2)]

    def scores(u):
        blk, hh = units[u]
        c0 = blk * ATT_SUB
        qt = qt_ref[0, :, c0:c0 + ATT_SUB]
        kwin = kcat[c0:c0 + ATT_WIN, :]
        pad_keys = jnp.where(i == 0, PAD - c0, 0) if c0 < PAD else None
        mask = first_head if hh == 0 else jnp.logical_not(first_head)
        qm = jnp.where(mask, qt, jnp.zeros_like(qt))
        m = None
        for r0 in range(0, ATT_WIN, ATT_KEY_ROWS):
            rows = slice(r0, r0 + ATT_KEY_ROWS)
            st = _dot(kwin[rows], qm) + tbl_ref[hh, rows, :]
            if pad_keys is not None:
                kj = r0 + lax.broadcasted_iota(jnp.int32, (ATT_KEY_ROWS, ATT_SUB), 0)
                st = jnp.where(kj >= pad_keys, st, NEG)
            st_s[u % ATT_SLOTS, rows, :] = st
            piece_max = jnp.max(st, axis=0, keepdims=True)
            m = piece_max if m is None else jnp.maximum(m, piece_max)
        return m

    def softmax_pv(u, m):
        blk, hh = units[u]
        c0 = blk * ATT_SUB
        p_s[u % ATT_SLOTS] = jnp.exp2((st_s[u % ATT_SLOTS] - m).astype(BF16))
        vt = vtcat[hh * VT_ROWS:(hh + 1) * VT_ROWS, c0:c0 + ATT_WIN]
        ot = _dot(vt, p_s[u % ATT_SLOTS])
        return ot[:HEAD_DIM] * (1.0 / ot[HEAD_DIM:HEAD_DIM + 1])

    outs = []
    ms = [scores(u) for u in range(ATT_LOOKAHEAD)]
    for u in range(len(units)):
        if u + ATT_LOOKAHEAD < len(units):
            ms.append(scores(u + ATT_LOOKAHEAD))
        outs.append(softmax_pv(u, ms[u]))
        if u % 2 == 1:
            c0 = units[u][0] * ATT_SUB
            pair = jnp.concatenate(outs[-2:], axis=0)
            o_ref[0, c0:c0 + ATT_SUB, :] = pair.T.astype(BF16)


def _attention(qt, k, vt, table):
    bsz, s, _ = k.shape
    t = ATT_TILE
    prev_idx = lambda i: jnp.maximum(i * (t // PAD) - 1, 0)
    return pl.pallas_call(
        _attn_kernel,
        grid=(HEAD_PAIRS, bsz, s // t),
        in_specs=[
            pl.BlockSpec((1, LANES, t), lambda p, b, i: (b, p, i)),
            pl.BlockSpec((1, PAD, LANES), lambda p, b, i: (b, prev_idx(i), p)),
            pl.BlockSpec((1, t, LANES), lambda p, b, i: (b, i, p)),
            pl.BlockSpec((1, LANES, PAD), lambda p, b, i: (b, p, prev_idx(i))),
            pl.BlockSpec((1, LANES, t), lambda p, b, i: (b, p, i)),
            pl.BlockSpec((2, ATT_WIN, ATT_SUB), lambda p, b, i: (p, 0, 0)),
        ],
        out_specs=pl.BlockSpec((1, t, LANES), lambda p, b, i: (b, i, p)),
        out_shape=jax.ShapeDtypeStruct(k.shape, BF16),
        scratch_shapes=[pltpu.VMEM((PAD + t, LANES), BF16), pltpu.VMEM((2 * VT_ROWS, PAD + t), BF16),
                        pltpu.VMEM((ATT_SLOTS, ATT_WIN, ATT_SUB), F32),
                        pltpu.VMEM((ATT_SLOTS, ATT_WIN, ATT_SUB), BF16)],
        compiler_params=pltpu.CompilerParams(
            dimension_semantics=("arbitrary", "arbitrary", "arbitrary"), vmem_limit_bytes=VMEM_LIMIT),
        name="band_attention",
    )(qt, k, k, vt, vt, table)


def kernel(x, a_norm, a_w_in, a_conv_w, a_conv_b, a_w_gate, a_b_gate, a_lambda, a_w_out, kv_norm, w_kv, k_norm, b_norm, b_w_q, b_q_norm, b_rel_bias, b_w_o, mlp_norm, w_up, w_down):
    bsz, s, d = x.shape
    assert d == D_MODEL and s % ATT_TILE == 0 and s % (L0_SUBS * REC_TILE) == 0 and s % KVQ_TILE == 0
    assert a_norm.shape[0] == 1 and b_norm.shape[0] == 1 and mlp_norm.shape[0] == 2
    n = bsz * s
    row = lambda p: p.reshape(1, -1).astype(F32)

    h = _layer0(x.reshape(n, d), s // (L0_SUBS * REC_TILE), row(a_norm[0]), a_w_in[0].astype(BF16), a_conv_w[0],
                row(a_conv_b[0]), a_w_gate[0].astype(BF16), a_b_gate[0], row(a_lambda[0]),
                a_w_out[0].astype(BF16), row(mlp_norm[0]), w_up[0].astype(BF16), w_down[0].astype(BF16))

    col = lambda g: jnp.broadcast_to(jnp.tile(g, N_HEADS).astype(F32)[:, None], (D_MODEL, LANES))
    q_gain = col(b_q_norm[0]) * (HEAD_DIM ** -0.5 * LOG2E)
    k_gain = col(k_norm)
    shp = (bsz, s, d)
    qt, k, vt = _kvq(h.reshape(shp), row(kv_norm), row(b_norm[0]), w_kv[:, :D_MODEL].T.astype(BF16),
                     w_kv[:, D_MODEL:].T.astype(BF16), b_w_q[0].T.astype(BF16), k_gain, q_gain)
    table = _bias_table(jnp.pad(b_rel_bias[0], ((0, 0), (0, 2 * LANES - NREL))))
    attn = _attention(qt, k, vt, table)
    out = _proj_mlp(attn.reshape(n, d), b_w_o[0].astype(BF16), h, row(mlp_norm[1]),
                    w_up[1].astype(BF16), w_down[1].astype(BF16))
    return out.reshape(shp)
```

```python
import functools

import jax
import jax.numpy as jnp
from jax import lax
from jax.experimental import pallas as pl
from jax.experimental.pallas import tpu as pltpu

D_MODEL = 1024
D_RNN = D_MODEL
LRU_BLOCKS = 8
LRU_BW = D_RNN // LRU_BLOCKS
CONV_W = 4
LRU_C = 8.0
N_HEADS = 16
HEAD_DIM = 64
CHUNK = 64
LEFT_CHUNKS = 8
PAD = LEFT_CHUNKS * CHUNK
MAX_REL = 2 * CHUNK
MIN_REL = -(CHUNK - 1)
NREL = MAX_REL - MIN_REL + 1
D_FF = 4 * D_MODEL
EPS = 1e-6

LANES = 128
SUBLANES = 8
TINY = 1.1754944e-38
NEG = -1e30

REC_TILE = 256
L0_SUBS = 2
L0_FF_CHUNK = 1024
L0_ORDER = ("rec", "conv", "g0", "m", "g1", "gate", "m", "s0", "s1", "m", "s2", "s3", "gelu", "m", "out")
MLP_TILE = 1024
KVQ_TILE = 1024
KVQ_SUB = 256
ATT_TILE = 4096
ATT_SUB = 256
ATT_WIN = PAD + ATT_SUB
ATT_LOOKAHEAD = 4
ATT_SLOTS = ATT_LOOKAHEAD + 2
ATT_KEY_ROWS = 384
VT_ROWS = HEAD_DIM + 16
LOG2E = 1.4426950408889634
HEAD_PAIRS = N_HEADS // 2
VMEM_LIMIT = 56 * 1024 * 1024

F32 = jnp.float32
BF16 = jnp.bfloat16


def _dot(a, b):
    return jnp.dot(a, b, preferred_element_type=F32)


def _rms_scale(x):
    return lax.rsqrt(jnp.mean(x * x, axis=-1, keepdims=True) + EPS)


def _const_spec(shape):
    zeros = (0,) * len(shape)
    return pl.BlockSpec(shape, lambda *_: zeros, pipeline_mode=pl.Buffered(1))


def _layer0_kernel(x_ref, norm_ref, w_in_ref, cw_ref, cb_ref, wg_ref, bg_ref, lam_ref, w_out_ref,
                   mnorm_ref, w_up_ref, w_down_ref, o_ref,
                   rec_ext, a_s, b_s, h_carry, mix_s, *, tiles_per_seq):
    t = REC_TILE
    j = pl.program_id(0)

    @pl.when(j % tiles_per_seq == 0)
    def _():
        rec_ext[:, 0:SUBLANES, :] = jnp.zeros((D_RNN // LANES, SUBLANES, LANES), F32)
        h_carry[...] = jnp.zeros_like(h_carry)

    @pl.when(j == 0)
    def _():
        mix_s[1] = jnp.zeros((L0_SUBS, t, D_MODEL), F32)

    slot = j % 2
    row = lax.broadcasted_iota(jnp.int32, (SUBLANES, D_RNN), 0)
    n_slabs = t // SUBLANES
    h_state = h_carry[...]
    for q in range(L0_SUBS):
        h_state = _layer0_sub_tile(
            q, slot, h_state, row, n_slabs, x_ref, norm_ref, w_in_ref, cw_ref, cb_ref, wg_ref, bg_ref,
            lam_ref, w_out_ref, mnorm_ref, w_up_ref, w_down_ref, o_ref, rec_ext, a_s, b_s, mix_s)
    h_carry[...] = h_state


def _layer0_sub_tile(q, slot, h_in, row, n_slabs, x_ref, norm_ref, w_in_ref, cw_ref, cb_ref, wg_ref,
                     bg_ref, lam_ref, w_out_ref, mnorm_ref, w_up_ref, w_down_ref, o_ref,
                     rec_ext, a_s, b_s, mix_s):
    t = REC_TILE
    rows_q = slice(q * t, (q + 1) * t)

    hp = mix_s[1 - slot, q]
    hn = (hp * _rms_scale(hp) * mnorm_ref[...]).astype(BF16)

    mlp_state = {"acc": hp, "k": 0}

    def mlp_chunk():
        k = mlp_state["k"]
        sl = slice(k * L0_FF_CHUNK, (k + 1) * L0_FF_CHUNK)
        up = jnp.maximum(_dot(hn, w_up_ref[:, sl]), 0.0)
        mlp_state["acc"] = mlp_state["acc"] + _dot((up * up).astype(BF16), w_down_ref[sl, :])
        mlp_state["k"] = k + 1

    x = x_ref[rows_q, :]
    xn = (x * _rms_scale(x) * norm_ref[...]).astype(BF16)
    v = {"h": h_in}

    def rec_proj():
        v["rec"] = _dot(xn, w_in_ref[:, D_RNN:])

    def gate_proj():
        v["gate"] = _dot(xn, w_in_ref[:, :D_RNN])

    def conv():
        cs = []
        for n in range(D_RNN // LANES):
            sl = slice(n * LANES, (n + 1) * LANES)
            rec = v["rec"][:, sl]
            rec_ext[n, SUBLANES:SUBLANES + t, :] = rec
            c = cb_ref[:, sl] + rec * cw_ref[CONV_W - 1:CONV_W, sl]
            for k in range(1, CONV_W):
                c = c + rec_ext[n, pl.ds(SUBLANES - k, t), :] * cw_ref[CONV_W - 1 - k:CONV_W - k, sl]
            cs.append(c)
            rec_ext[n, 0:SUBLANES, :] = rec_ext[n, t:t + SUBLANES, :]
        v["c"] = cs

    def gates(half):
        neg_lam = -lam_ref[...]
        softplus = jnp.maximum(neg_lam, 0.0) + jnp.log1p(jnp.exp(-jnp.abs(neg_lam)))
        for n in range(half * LRU_BLOCKS // 2, (half + 1) * LRU_BLOCKS // 2):
            sl = slice(n * LRU_BW, (n + 1) * LRU_BW)
            cb = v["c"][n]
            g = _dot(cb.astype(BF16), wg_ref[n]) + bg_ref[n:n + 1, :]
            r = jax.nn.sigmoid(g[:, :LRU_BW])
            i = jax.nn.sigmoid(g[:, LRU_BW:])
            log_a = (-LRU_C) * r * softplus[:, sl]
            a = jnp.exp(log_a)
            z = 1.0 - a * a
            mult = z * lax.rsqrt(jnp.maximum(z, TINY))
            a_s[q, :, sl] = a
            b_s[q, :, sl] = mult * (i * cb)

    def scan(quarter):
        for s in range(quarter * n_slabs // 4, (quarter + 1) * n_slabs // 4):
            rows = slice(s * SUBLANES, (s + 1) * SUBLANES)
            a = a_s[q, rows, :]
            b = b_s[q, rows, :]
            for d in (1, 2, 4):
                keep = row >= d
                a_sh = jnp.where(keep, pltpu.roll(a, d, 0), 1.0)
                b_sh = jnp.where(keep, pltpu.roll(b, d, 0), 0.0)
                b = a * b_sh + b
                a = a * a_sh
            h = a * v["h"] + b
            b_s[q, rows, :] = h
            v["h"] = h[SUBLANES - 1:SUBLANES, :]

    def gelu_gate():
        v["y"] = (jax.nn.gelu(v["gate"]) * b_s[q]).astype(BF16)

    def out_proj():
        mix_s[slot, q] = x + _dot(v["y"], w_out_ref[...])

    phases = {
        "rec": rec_proj, "gate": gate_proj, "conv": conv, "g0": lambda: gates(0), "g1": lambda: gates(1),
        "s0": lambda: scan(0), "s1": lambda: scan(1), "s2": lambda: scan(2), "s3": lambda: scan(3),
        "gelu": gelu_gate, "out": out_proj, "m": mlp_chunk,
    }
    for name in L0_ORDER:
        phases[name]()
    assert mlp_state["k"] == D_FF // L0_FF_CHUNK
    o_ref[rows_q, :] = mlp_state["acc"]
    return v["h"]


def _layer0(x2d, tiles_per_seq, norm, w_in, conv_w, conv_b, w_gate, b_gate, lam, w_out, mlp_norm, w_up, w_down):
    t = REC_TILE
    blk = L0_SUBS * t
    n_tiles = x2d.shape[0] // blk
    return pl.pallas_call(
        functools.partial(_layer0_kernel, tiles_per_seq=tiles_per_seq),
        grid=(n_tiles + 1,),
        in_specs=[
            pl.BlockSpec((blk, D_MODEL), lambda j: (jnp.minimum(j, n_tiles - 1), 0)),
            _const_spec((1, D_MODEL)),
            _const_spec((D_MODEL, 2 * D_RNN)),
            _const_spec((CONV_W, D_RNN)),
            _const_spec((1, D_RNN)),
            _const_spec((LRU_BLOCKS, LRU_BW, 2 * LRU_BW)),
            _const_spec((LRU_BLOCKS, 2 * LRU_BW)),
            _const_spec((1, D_RNN)),
            _const_spec((D_RNN, D_MODEL)),
            _const_spec((1, D_MODEL)),
            _const_spec((D_MODEL, D_FF)),
            _const_spec((D_FF, D_MODEL)),
        ],
        out_specs=pl.BlockSpec((blk, D_MODEL), lambda j: (jnp.maximum(j - 1, 0), 0)),
        out_shape=jax.ShapeDtypeStruct(x2d.shape, F32),
        scratch_shapes=[
            pltpu.VMEM((D_RNN // LANES, t + SUBLANES, LANES), F32),
            pltpu.VMEM((L0_SUBS, t, D_RNN), F32),
            pltpu.VMEM((L0_SUBS, t, D_RNN), F32),
            pltpu.VMEM((1, D_RNN), F32),
            pltpu.VMEM((2, L0_SUBS, t, D_MODEL), F32),
        ],
        compiler_params=pltpu.CompilerParams(
            dimension_semantics=("arbitrary",), vmem_limit_bytes=VMEM_LIMIT),
        name="layer0_mixer_mlp",
    )(x2d, norm, w_in, conv_w, conv_b, w_gate, b_gate, lam, w_out, mlp_norm, w_up, w_down)


FF_CHUNK = 1024


def _mlp_body(h, norm_ref, w_up_ref, w_down_ref):
    hn = (h * _rms_scale(h) * norm_ref[...]).astype(BF16)
    acc = h
    for j in range(D_FF // FF_CHUNK):
        sl = slice(j * FF_CHUNK, (j + 1) * FF_CHUNK)
        up = jnp.maximum(_dot(hn, w_up_ref[:, sl]), 0.0)
        acc = acc + _dot((up * up).astype(BF16), w_down_ref[sl, :])
    return acc


def _proj_mlp_kernel(a_ref, w_o_ref, x_ref, norm_ref, w_up_ref, w_down_ref, o_ref):
    h = x_ref[...] + _dot(a_ref[...], w_o_ref[...])
    o_ref[...] = _mlp_body(h, norm_ref, w_up_ref, w_down_ref)


def _proj_mlp(attn, w_o, x2d, norm, w_up, w_down):
    n = x2d.shape[0]
    t = MLP_TILE
    tile = pl.BlockSpec((t, D_MODEL), lambda i: (i, 0))
    return pl.pallas_call(
        _proj_mlp_kernel,
        grid=(n // t,),
        in_specs=[tile, _const_spec((D_MODEL, D_MODEL)), tile, _const_spec((1, D_MODEL)),
                  _const_spec((D_MODEL, D_FF)), _const_spec((D_FF, D_MODEL))],
        out_specs=tile,
        out_shape=jax.ShapeDtypeStruct(x2d.shape, F32),
        compiler_params=pltpu.CompilerParams(
            dimension_semantics=("arbitrary",), vmem_limit_bytes=VMEM_LIMIT),
        name="proj_mlp",
    )(attn, w_o, x2d, norm, w_up, w_down)


def _dot_nt(a, b):
    return lax.dot_general(a, b, (((1,), (1,)), ((), ())), preferred_element_type=F32)


def _head_rms_normalise_t(zt, gain_ref):
    t = zt.shape[1]
    z3 = zt.reshape(N_HEADS, HEAD_DIM, t)
    inv = lax.rsqrt(jnp.mean(z3 * z3, axis=1, keepdims=True) + EPS)
    gain = jnp.concatenate([gain_ref[...]] * (t // LANES), axis=1)
    return (z3 * inv).reshape(N_HEADS * HEAD_DIM, t) * gain


def _kvq_kernel(x_ref, kvn_ref, qn_ref, w_kt_ref, w_vt_ref, w_qt_ref, kg_ref, qg_ref,
                qt_ref, k_ref, vt_ref):
    def project(r0):
        x = x_ref[0, r0:r0 + KVQ_SUB, :]
        xs = x * _rms_scale(x)
        xkv = (xs * kvn_ref[...]).astype(BF16)
        xq = (xs * qn_ref[...]).astype(BF16)
        return _dot_nt(w_vt_ref[...], xkv), _dot_nt(w_kt_ref[...], xkv), _dot_nt(w_qt_ref[...], xq)

    def finish(r0, vt, kt, qt):
        rows = slice(r0, r0 + KVQ_SUB)
        vt_ref[0, :, rows] = vt.astype(BF16)
        k_ref[0, rows, :] = _head_rms_normalise_t(kt, kg_ref).T.astype(BF16)
        qt_ref[0, :, rows] = _head_rms_normalise_t(qt, qg_ref).astype(BF16)

    starts = list(range(0, KVQ_TILE, KVQ_SUB))
    pending = project(starts[0])
    for n, r0 in enumerate(starts):
        nxt = project(starts[n + 1]) if n + 1 < len(starts) else None
        finish(r0, *pending)
        pending = nxt


def _kvq(x, kv_norm, q_norm, w_kt, w_vt, w_qt, k_gain, q_gain):
    bsz, s, _ = x.shape
    t = KVQ_TILE
    tok_major = pl.BlockSpec((1, t, D_MODEL), lambda b, i: (b, i, 0))
    feat_major = pl.BlockSpec((1, D_MODEL, t), lambda b, i: (b, 0, i))
    w_spec = _const_spec((D_MODEL, D_MODEL))
    g_spec = _const_spec((D_MODEL, LANES))
    ft = jax.ShapeDtypeStruct((bsz, D_MODEL, s), BF16)
    return pl.pallas_call(
        _kvq_kernel,
        grid=(bsz, s // t),
        in_specs=[tok_major, _const_spec((1, D_MODEL)), _const_spec((1, D_MODEL)),
                  w_spec, w_spec, w_spec, g_spec, g_spec],
        out_specs=[feat_major, tok_major, feat_major],
        out_shape=[ft, jax.ShapeDtypeStruct(x.shape, BF16), ft],
        compiler_params=pltpu.CompilerParams(
            dimension_semantics=("arbitrary", "arbitrary"), vmem_limit_bytes=VMEM_LIMIT),
        name="kvq_proj",
    )(x, kv_norm, q_norm, w_kt, w_vt, w_qt, k_gain, q_gain)


BASE_W = ATT_WIN + ATT_SUB
N_VARIANTS = 1 + PAD // ATT_SUB
ROLL_ROWS = 128


def _bias_kernel(rb_ref, o_ref, base_s):
    rb = rb_ref[...] * LOG2E
    r_idx = lax.broadcasted_iota(jnp.int32, (2 * LANES, BASE_W), 0)
    m_idx = lax.broadcasted_iota(jnp.int32, (2 * LANES, BASE_W), 1)
    want = jnp.clip(m_idx - ATT_SUB, MIN_REL, MAX_REL) - MIN_REL
    onehot = (r_idx == want).astype(BF16)
    p0 = rb.astype(BF16)
    r1 = rb - p0.astype(F32)
    p1 = r1.astype(BF16)
    p2 = (r1 - p1.astype(F32)).astype(BF16)
    base_s[...] = _dot(p0, onehot) + _dot(p1, onehot) + _dot(p2, onehot)
    base = base_s[pl.ds(pl.program_id(0), 1), :]

    for r0 in range(0, ATT_WIN, ROLL_ROWS):
        kj = r0 + lax.broadcasted_iota(jnp.int32, (ROLL_ROWS, ATT_SUB), 0)
        qi = lax.broadcasted_iota(jnp.int32, (ROLL_ROWS, ATT_SUB), 1)
        band_lo = (qi // CHUNK) * CHUNK
        in_band = (kj >= band_lo) & (kj < band_lo + PAD + CHUNK)
        rows = jnp.broadcast_to(base, (ROLL_ROWS, BASE_W))
        toeplitz = pltpu.roll(rows, r0, 1, stride=1, stride_axis=0)
        tile = jnp.where(in_band, toeplitz[:, ATT_WIN:], NEG)
        o_ref[0, 0, r0:r0 + ROLL_ROWS, :] = tile
        for v in range(1, N_VARIANTS):
            o_ref[v, 0, r0:r0 + ROLL_ROWS, :] = jnp.where(kj >= PAD - (v - 1) * ATT_SUB, tile, NEG)


def _bias_table(rel_bias_padded):
    return pl.pallas_call(
        _bias_kernel,
        grid=(N_HEADS,),
        in_specs=[_const_spec((N_HEADS, 2 * LANES))],
        out_specs=pl.BlockSpec((N_VARIANTS, 1, ATT_WIN, ATT_SUB), lambda h: (0, h, 0, 0)),
        out_shape=jax.ShapeDtypeStruct((N_VARIANTS, N_HEADS, ATT_WIN, ATT_SUB), F32),
        scratch_shapes=[pltpu.VMEM((N_HEADS, BASE_W), F32)],
        compiler_params=pltpu.CompilerParams(
            dimension_semantics=("arbitrary",), vmem_limit_bytes=VMEM_LIMIT),
        name="rel_bias_table",
    )(rel_bias_padded)


def _attn_kernel(qt_ref, kp_ref, kc_ref, vtp_ref, vtc_ref, tbl_ref, o_ref, kcat, vtcat, st_s, p_s):
    i = pl.program_id(2)
    kcat[0:PAD, :] = kp_ref[0]
    kcat[PAD:, :] = kc_ref[0]
    for hh in range(2):
        src = slice(hh * HEAD_DIM, (hh + 1) * HEAD_DIM)
        r0 = hh * VT_ROWS
        vtcat[r0:r0 + HEAD_DIM, 0:PAD] = vtp_ref[0, src, :]
        vtcat[r0:r0 + HEAD_DIM, PAD:] = vtc_ref[0, src, :]
        vtcat[r0 + HEAD_DIM:r0 + VT_ROWS, :] = jnp.ones((VT_ROWS - HEAD_DIM, PAD + ATT_TILE), BF16)

    feat = lax.broadcasted_iota(jnp.int32, (LANES, ATT_SUB), 0)
    first_head = feat < HEAD_DIM
    n_blk = ATT_TILE // ATT_SUB
    units = [(blk, hh) for blk in range(n_blk) for hh in range(2)]

    def scores(u):
        blk, hh = units[u]
        c0 = blk * ATT_SUB
        qt = qt_ref[0, :, c0:c0 + ATT_SUB]
        kwin = kcat[c0:c0 + ATT_WIN, :]
        variant = jnp.where(i == 0, blk + 1, 0) if blk + 1 < N_VARIANTS else 0
        mask = first_head if hh == 0 else jnp.logical_not(first_head)
        qm = jnp.where(mask, qt, jnp.zeros_like(qt))
        m = None
        for r0 in range(0, ATT_WIN, ATT_KEY_ROWS):
            rows = slice(r0, r0 + ATT_KEY_ROWS)
            st = _dot(kwin[rows], qm) + tbl_ref[variant, hh, rows, :]
            st_s[u % ATT_SLOTS, rows, :] = st
            piece_max = jnp.max(st, axis=0, keepdims=True)
            m = piece_max if m is None else jnp.maximum(m, piece_max)
        return m

    def softmax_pv(u, m):
        blk, hh = units[u]
        c0 = blk * ATT_SUB
        p_s[u % ATT_SLOTS] = jnp.exp2((st_s[u % ATT_SLOTS] - m).astype(BF16))
        vt = vtcat[hh * VT_ROWS:(hh + 1) * VT_ROWS, c0:c0 + ATT_WIN]
        ot = _dot(vt, p_s[u % ATT_SLOTS])
        return ot[:HEAD_DIM] * (1.0 / ot[HEAD_DIM:HEAD_DIM + 1])

    outs = []
    ms = [scores(u) for u in range(ATT_LOOKAHEAD)]
    for u in range(len(units)):
        if u + ATT_LOOKAHEAD < len(units):
            ms.append(scores(u + ATT_LOOKAHEAD))
        outs.append(softmax_pv(u, ms[u]))
        if u % 2 == 1:
            c0 = units[u][0] * ATT_SUB
            pair = jnp.concatenate(outs[-2:], axis=0)
            o_ref[0, c0:c0 + ATT_SUB, :] = pair.T.astype(BF16)


def _attention(qt, k, vt, table):
    bsz, s, _ = k.shape
    t = ATT_TILE
    prev_idx = lambda i: jnp.maximum(i * (t // PAD) - 1, 0)
    return pl.pallas_call(
        _attn_kernel,
        grid=(HEAD_PAIRS, bsz, s // t),
        in_specs=[
            pl.BlockSpec((1, LANES, t), lambda p, b, i: (b, p, i)),
            pl.BlockSpec((1, PAD, LANES), lambda p, b, i: (b, prev_idx(i), p)),
            pl.BlockSpec((1, t, LANES), lambda p, b, i: (b, i, p)),
            pl.BlockSpec((1, LANES, PAD), lambda p, b, i: (b, p, prev_idx(i))),
            pl.BlockSpec((1, LANES, t), lambda p, b, i: (b, p, i)),
            pl.BlockSpec((N_VARIANTS, 2, ATT_WIN, ATT_SUB), lambda p, b, i: (0, p, 0, 0)),
        ],
        out_specs=pl.BlockSpec((1, t, LANES), lambda p, b, i: (b, i, p)),
        out_shape=jax.ShapeDtypeStruct(k.shape, BF16),
        scratch_shapes=[pltpu.VMEM((PAD + t, LANES), BF16), pltpu.VMEM((2 * VT_ROWS, PAD + t), BF16),
                        pltpu.VMEM((ATT_SLOTS, ATT_WIN, ATT_SUB), F32),
                        pltpu.VMEM((ATT_SLOTS, ATT_WIN, ATT_SUB), BF16)],
        compiler_params=pltpu.CompilerParams(
            dimension_semantics=("arbitrary", "arbitrary", "arbitrary"), vmem_limit_bytes=VMEM_LIMIT),
        name="band_attention",
    )(qt, k, k, vt, vt, table)


def kernel(x, a_norm, a_w_in, a_conv_w, a_conv_b, a_w_gate, a_b_gate, a_lambda, a_w_out, kv_norm, w_kv, k_norm, b_norm, b_w_q, b_q_norm, b_rel_bias, b_w_o, mlp_norm, w_up, w_down):
    bsz, s, d = x.shape
    assert d == D_MODEL and s % ATT_TILE == 0 and s % (L0_SUBS * REC_TILE) == 0 and s % KVQ_TILE == 0
    assert a_norm.shape[0] == 1 and b_norm.shape[0] == 1 and mlp_norm.shape[0] == 2
    n = bsz * s
    row = lambda p: p.reshape(1, -1).astype(F32)

    h = _layer0(x.reshape(n, d), s // (L0_SUBS * REC_TILE), row(a_norm[0]), a_w_in[0].astype(BF16), a_conv_w[0],
                row(a_conv_b[0]), a_w_gate[0].astype(BF16), a_b_gate[0], row(a_lambda[0]),
                a_w_out[0].astype(BF16), row(mlp_norm[0]), w_up[0].astype(BF16), w_down[0].astype(BF16))

    col = lambda g: jnp.broadcast_to(jnp.tile(g, N_HEADS).astype(F32)[:, None], (D_MODEL, LANES))
    q_gain = col(b_q_norm[0]) * (HEAD_DIM ** -0.5 * LOG2E)
    k_gain = col(k_norm)
    shp = (bsz, s, d)
    qt, k, vt = _kvq(h.reshape(shp), row(kv_norm), row(b_norm[0]), w_kv[:, :D_MODEL].T.astype(BF16),
                     w_kv[:, D_MODEL:].T.astype(BF16), b_w_q[0].T.astype(BF16), k_gain, q_gain)
    table = _bias_table(jnp.pad(b_rel_bias[0], ((0, 0), (0, 2 * LANES - NREL))))
    attn = _attention(qt, k, vt, table)
    out = _proj_mlp(attn.reshape(n, d), b_w_o[0].astype(BF16), h, row(mlp_norm[1]),
                    w_up[1].astype(BF16), w_down[1].astype(BF16))
    return out.reshape(shp)
```

```python
import functools

import jax
import jax.numpy as jnp
from jax import lax
from jax.experimental import pallas as pl
from jax.experimental.pallas import tpu as pltpu

D_MODEL = 1024
D_RNN = D_MODEL
LRU_BLOCKS = 8
LRU_BW = D_RNN // LRU_BLOCKS
CONV_W = 4
LRU_C = 8.0
N_HEADS = 16
HEAD_DIM = 64
CHUNK = 64
LEFT_CHUNKS = 8
PAD = LEFT_CHUNKS * CHUNK
MAX_REL = 2 * CHUNK
MIN_REL = -(CHUNK - 1)
NREL = MAX_REL - MIN_REL + 1
D_FF = 4 * D_MODEL
EPS = 1e-6

LANES = 128
SUBLANES = 8
TINY = 1.1754944e-38
NEG = -1e30

REC_TILE = 256
L0_SUBS = 2
L0_FF_CHUNK = 1024
L0_ORDER = ("rec", "conv", "g0", "m", "g1", "gate", "m", "s0", "s1", "m", "s2", "s3", "gelu", "m", "out")
MLP_TILE = 1024
KVQ_TILE = 1024
KVQ_SUB = 512
ATT_TILE = 8192
ATT_SUB = 256
ATT_WIN = PAD + ATT_SUB
ATT_LOOKAHEAD = 4
ATT_SLOTS = ATT_LOOKAHEAD + 2
ATT_KEY_ROWS = 384
VT_ROWS = HEAD_DIM + 16
LOG2E = 1.4426950408889634
HEAD_PAIRS = N_HEADS // 2
VMEM_LIMIT = 56 * 1024 * 1024

F32 = jnp.float32
BF16 = jnp.bfloat16


def _dot(a, b):
    return jnp.dot(a, b, preferred_element_type=F32)


def _rms_scale(x):
    return lax.rsqrt(jnp.mean(x * x, axis=-1, keepdims=True) + EPS)


def _const_spec(shape):
    zeros = (0,) * len(shape)
    return pl.BlockSpec(shape, lambda *_: zeros, pipeline_mode=pl.Buffered(1))


def _layer0_kernel(x_ref, norm_ref, w_in_ref, cw_ref, cb_ref, wg_ref, bg_ref, lam_ref, w_out_ref,
                   mnorm_ref, w_up_ref, w_down_ref, o_ref,
                   rec_ext, a_s, b_s, h_carry, mix_s, *, tiles_per_seq):
    t = REC_TILE
    j = pl.program_id(0)

    @pl.when(j % tiles_per_seq == 0)
    def _():
        rec_ext[:, 0:SUBLANES, :] = jnp.zeros((D_RNN // LANES, SUBLANES, LANES), F32)
        h_carry[...] = jnp.zeros_like(h_carry)

    @pl.when(j == 0)
    def _():
        mix_s[1] = jnp.zeros((L0_SUBS, t, D_MODEL), F32)

    slot = j % 2
    row = lax.broadcasted_iota(jnp.int32, (SUBLANES, D_RNN), 0)
    n_slabs = t // SUBLANES
    h_state = h_carry[...]
    for q in range(L0_SUBS):
        h_state = _layer0_sub_tile(
            q, slot, h_state, row, n_slabs, x_ref, norm_ref, w_in_ref, cw_ref, cb_ref, wg_ref, bg_ref,
            lam_ref, w_out_ref, mnorm_ref, w_up_ref, w_down_ref, o_ref, rec_ext, a_s, b_s, mix_s)
    h_carry[...] = h_state


def _layer0_sub_tile(q, slot, h_in, row, n_slabs, x_ref, norm_ref, w_in_ref, cw_ref, cb_ref, wg_ref,
                     bg_ref, lam_ref, w_out_ref, mnorm_ref, w_up_ref, w_down_ref, o_ref,
                     rec_ext, a_s, b_s, mix_s):
    t = REC_TILE
    rows_q = slice(q * t, (q + 1) * t)

    hp = mix_s[1 - slot, q]
    hn = (hp * _rms_scale(hp) * mnorm_ref[...]).astype(BF16)

    mlp_state = {"acc": hp, "k": 0}

    def mlp_chunk():
        k = mlp_state["k"]
        sl = slice(k * L0_FF_CHUNK, (k + 1) * L0_FF_CHUNK)
        up = jnp.maximum(_dot(hn, w_up_ref[:, sl]), 0.0)
        mlp_state["acc"] = mlp_state["acc"] + _dot((up * up).astype(BF16), w_down_ref[sl, :])
        mlp_state["k"] = k + 1

    x = x_ref[rows_q, :]
    xn = (x * _rms_scale(x) * norm_ref[...]).astype(BF16)
    v = {"h": h_in}

    def rec_proj():
        v["rec"] = _dot(xn, w_in_ref[:, D_RNN:])

    def gate_proj():
        v["gate"] = _dot(xn, w_in_ref[:, :D_RNN])

    def conv():
        cs = []
        for n in range(D_RNN // LANES):
            sl = slice(n * LANES, (n + 1) * LANES)
            rec = v["rec"][:, sl]
            rec_ext[n, SUBLANES:SUBLANES + t, :] = rec
            c = cb_ref[:, sl] + rec * cw_ref[CONV_W - 1:CONV_W, sl]
            for k in range(1, CONV_W):
                c = c + rec_ext[n, pl.ds(SUBLANES - k, t), :] * cw_ref[CONV_W - 1 - k:CONV_W - k, sl]
            cs.append(c)
            rec_ext[n, 0:SUBLANES, :] = rec_ext[n, t:t + SUBLANES, :]
        v["c"] = cs

    def gates(half):
        neg_lam = -lam_ref[...]
        softplus = jnp.maximum(neg_lam, 0.0) + jnp.log1p(jnp.exp(-jnp.abs(neg_lam)))
        for n in range(half * LRU_BLOCKS // 2, (half + 1) * LRU_BLOCKS // 2):
            sl = slice(n * LRU_BW, (n + 1) * LRU_BW)
            cb = v["c"][n]
            g = _dot(cb.astype(BF16), wg_ref[n]) + bg_ref[n:n + 1, :]
            r = jax.nn.sigmoid(g[:, :LRU_BW])
            i = jax.nn.sigmoid(g[:, LRU_BW:])
            log_a = (-LRU_C) * r * softplus[:, sl]
            a = jnp.exp(log_a)
            z = 1.0 - a * a
            mult = z * lax.rsqrt(jnp.maximum(z, TINY))
            a_s[q, :, sl] = a
            b_s[q, :, sl] = mult * (i * cb)

    def scan(quarter):
        for s in range(quarter * n_slabs // 4, (quarter + 1) * n_slabs // 4):
            rows = slice(s * SUBLANES, (s + 1) * SUBLANES)
            a = a_s[q, rows, :]
            b = b_s[q, rows, :]
            for d in (1, 2, 4):
                keep = row >= d
                a_sh = jnp.where(keep, pltpu.roll(a, d, 0), 1.0)
                b_sh = jnp.where(keep, pltpu.roll(b, d, 0), 0.0)
                b = a * b_sh + b
                a = a * a_sh
            h = a * v["h"] + b
            b_s[q, rows, :] = h
            v["h"] = h[SUBLANES - 1:SUBLANES, :]

    def gelu_gate():
        v["y"] = (jax.nn.gelu(v["gate"]) * b_s[q]).astype(BF16)

    def out_proj():
        mix_s[slot, q] = x + _dot(v["y"], w_out_ref[...])

    phases = {
        "rec": rec_proj, "gate": gate_proj, "conv": conv, "g0": lambda: gates(0), "g1": lambda: gates(1),
        "s0": lambda: scan(0), "s1": lambda: scan(1), "s2": lambda: scan(2), "s3": lambda: scan(3),
        "gelu": gelu_gate, "out": out_proj, "m": mlp_chunk,
    }
    for name in L0_ORDER:
        phases[name]()
    assert mlp_state["k"] == D_FF // L0_FF_CHUNK
    o_ref[rows_q, :] = mlp_state["acc"]
    return v["h"]


def _layer0(x2d, tiles_per_seq, norm, w_in, conv_w, conv_b, w_gate, b_gate, lam, w_out, mlp_norm, w_up, w_down):
    t = REC_TILE
    blk = L0_SUBS * t
    n_tiles = x2d.shape[0] // blk
    return pl.pallas_call(
        functools.partial(_layer0_kernel, tiles_per_seq=tiles_per_seq),
        grid=(n_tiles + 1,),
        in_specs=[
            pl.BlockSpec((blk, D_MODEL), lambda j: (jnp.minimum(j, n_tiles - 1), 0)),
            _const_spec((1, D_MODEL)),
            _const_spec((D_MODEL, 2 * D_RNN)),
            _const_spec((CONV_W, D_RNN)),
            _const_spec((1, D_RNN)),
            _const_spec((LRU_BLOCKS, LRU_BW, 2 * LRU_BW)),
            _const_spec((LRU_BLOCKS, 2 * LRU_BW)),
            _const_spec((1, D_RNN)),
            _const_spec((D_RNN, D_MODEL)),
            _const_spec((1, D_MODEL)),
            _const_spec((D_MODEL, D_FF)),
            _const_spec((D_FF, D_MODEL)),
        ],
        out_specs=pl.BlockSpec((blk, D_MODEL), lambda j: (jnp.maximum(j - 1, 0), 0)),
        out_shape=jax.ShapeDtypeStruct(x2d.shape, F32),
        scratch_shapes=[
            pltpu.VMEM((D_RNN // LANES, t + SUBLANES, LANES), F32),
            pltpu.VMEM((L0_SUBS, t, D_RNN), F32),
            pltpu.VMEM((L0_SUBS, t, D_RNN), F32),
            pltpu.VMEM((1, D_RNN), F32),
            pltpu.VMEM((2, L0_SUBS, t, D_MODEL), F32),
        ],
        compiler_params=pltpu.CompilerParams(
            dimension_semantics=("arbitrary",), vmem_limit_bytes=VMEM_LIMIT),
        name="layer0_mixer_mlp",
    )(x2d, norm, w_in, conv_w, conv_b, w_gate, b_gate, lam, w_out, mlp_norm, w_up, w_down)


FF_CHUNK = 1024


def _mlp_body(h, norm_ref, w_up_ref, w_down_ref):
    hn = (h * _rms_scale(h) * norm_ref[...]).astype(BF16)
    acc = h
    for j in range(D_FF // FF_CHUNK):
        sl = slice(j * FF_CHUNK, (j + 1) * FF_CHUNK)
        up = jnp.maximum(_dot(hn, w_up_ref[:, sl]), 0.0)
        acc = acc + _dot((up * up).astype(BF16), w_down_ref[sl, :])
    return acc


def _proj_mlp_kernel(a_ref, w_o_ref, x_ref, norm_ref, w_up_ref, w_down_ref, o_ref):
    h = x_ref[...] + _dot(a_ref[...], w_o_ref[...])
    o_ref[...] = _mlp_body(h, norm_ref, w_up_ref, w_down_ref)


def _proj_mlp(attn, w_o, x2d, norm, w_up, w_down):
    n = x2d.shape[0]
    t = MLP_TILE
    tile = pl.BlockSpec((t, D_MODEL), lambda i: (i, 0))
    return pl.pallas_call(
        _proj_mlp_kernel,
        grid=(n // t,),
        in_specs=[tile, _const_spec((D_MODEL, D_MODEL)), tile, _const_spec((1, D_MODEL)),
                  _const_spec((D_MODEL, D_FF)), _const_spec((D_FF, D_MODEL))],
        out_specs=tile,
        out_shape=jax.ShapeDtypeStruct(x2d.shape, F32),
        compiler_params=pltpu.CompilerParams(
            dimension_semantics=("arbitrary",), vmem_limit_bytes=VMEM_LIMIT),
        name="proj_mlp",
    )(attn, w_o, x2d, norm, w_up, w_down)


def _dot_nt(a, b):
    return lax.dot_general(a, b, (((1,), (1,)), ((), ())), preferred_element_type=F32)


def _head_rms_normalise_t(zt, gain_ref):
    t = zt.shape[1]
    z3 = zt.reshape(N_HEADS, HEAD_DIM, t)
    inv = lax.rsqrt(jnp.mean(z3 * z3, axis=1, keepdims=True) + EPS)
    gain = jnp.concatenate([gain_ref[...]] * (t // LANES), axis=1)
    return (z3 * inv).reshape(N_HEADS * HEAD_DIM, t) * gain


def _kvq_kernel(x_ref, kvn_ref, qn_ref, w_kt_ref, w_vt_ref, w_qt_ref, kg_ref, qg_ref,
                qt_ref, k_ref, vt_ref):
    def project(r0):
        x = x_ref[0, r0:r0 + KVQ_SUB, :]
        xs = x * _rms_scale(x)
        xkv = (xs * kvn_ref[...]).astype(BF16)
        xq = (xs * qn_ref[...]).astype(BF16)
        return _dot_nt(w_vt_ref[...], xkv), _dot_nt(w_kt_ref[...], xkv), _dot_nt(w_qt_ref[...], xq)

    def finish(r0, vt, kt, qt):
        rows = slice(r0, r0 + KVQ_SUB)
        vt_ref[0, :, rows] = vt.astype(BF16)
        k_ref[0, rows, :] = _head_rms_normalise_t(kt, kg_ref).T.astype(BF16)
        qt_ref[0, :, rows] = _head_rms_normalise_t(qt, qg_ref).astype(BF16)

    starts = list(range(0, KVQ_TILE, KVQ_SUB))
    pending = project(starts[0])
    for n, r0 in enumerate(starts):
        nxt = project(starts[n + 1]) if n + 1 < len(starts) else None
        finish(r0, *pending)
        pending = nxt


def _kvq(x, kv_norm, q_norm, w_kt, w_vt, w_qt, k_gain, q_gain):
    bsz, s, _ = x.shape
    t = KVQ_TILE
    tok_major = pl.BlockSpec((1, t, D_MODEL), lambda b, i: (b, i, 0))
    feat_major = pl.BlockSpec((1, D_MODEL, t), lambda b, i: (b, 0, i))
    w_spec = _const_spec((D_MODEL, D_MODEL))
    g_spec = _const_spec((D_MODEL, LANES))
    ft = jax.ShapeDtypeStruct((bsz, D_MODEL, s), BF16)
    return pl.pallas_call(
        _kvq_kernel,
        grid=(bsz, s // t),
        in_specs=[tok_major, _const_spec((1, D_MODEL)), _const_spec((1, D_MODEL)),
                  w_spec, w_spec, w_spec, g_spec, g_spec],
        out_specs=[feat_major, tok_major, feat_major],
        out_shape=[ft, jax.ShapeDtypeStruct(x.shape, BF16), ft],
        compiler_params=pltpu.CompilerParams(
            dimension_semantics=("arbitrary", "arbitrary"), vmem_limit_bytes=VMEM_LIMIT),
        name="kvq_proj",
    )(x, kv_norm, q_norm, w_kt, w_vt, w_qt, k_gain, q_gain)


BASE_W = ATT_WIN + ATT_SUB
N_VARIANTS = 1 + PAD // ATT_SUB
ROLL_ROWS = 128


def _bias_kernel(rb_ref, o_ref, base_s):
    rb = rb_ref[...] * LOG2E
    r_idx = lax.broadcasted_iota(jnp.int32, (2 * LANES, BASE_W), 0)
    m_idx = lax.broadcasted_iota(jnp.int32, (2 * LANES, BASE_W), 1)
    want = jnp.clip(m_idx - ATT_SUB, MIN_REL, MAX_REL) - MIN_REL
    onehot = (r_idx == want).astype(BF16)
    p0 = rb.astype(BF16)
    r1 = rb - p0.astype(F32)
    p1 = r1.astype(BF16)
    p2 = (r1 - p1.astype(F32)).astype(BF16)
    base_s[...] = _dot(p0, onehot) + _dot(p1, onehot) + _dot(p2, onehot)
    base = base_s[pl.ds(pl.program_id(0), 1), :]

    for r0 in range(0, ATT_WIN, ROLL_ROWS):
        kj = r0 + lax.broadcasted_iota(jnp.int32, (ROLL_ROWS, ATT_SUB), 0)
        qi = lax.broadcasted_iota(jnp.int32, (ROLL_ROWS, ATT_SUB), 1)
        band_lo = (qi // CHUNK) * CHUNK
        in_band = (kj >= band_lo) & (kj < band_lo + PAD + CHUNK)
        rows = jnp.broadcast_to(base, (ROLL_ROWS, BASE_W))
        toeplitz = pltpu.roll(rows, r0, 1, stride=1, stride_axis=0)
        tile = jnp.where(in_band, toeplitz[:, ATT_WIN:], NEG)
        o_ref[0, 0, r0:r0 + ROLL_ROWS, :] = tile
        for v in range(1, N_VARIANTS):
            o_ref[v, 0, r0:r0 + ROLL_ROWS, :] = jnp.where(kj >= PAD - (v - 1) * ATT_SUB, tile, NEG)


def _bias_table(rel_bias_padded):
    return pl.pallas_call(
        _bias_kernel,
        grid=(N_HEADS,),
        in_specs=[_const_spec((N_HEADS, 2 * LANES))],
        out_specs=pl.BlockSpec((N_VARIANTS, 1, ATT_WIN, ATT_SUB), lambda h: (0, h, 0, 0)),
        out_shape=jax.ShapeDtypeStruct((N_VARIANTS, N_HEADS, ATT_WIN, ATT_SUB), F32),
        scratch_shapes=[pltpu.VMEM((N_HEADS, BASE_W), F32)],
        compiler_params=pltpu.CompilerParams(
            dimension_semantics=("arbitrary",), vmem_limit_bytes=VMEM_LIMIT),
        name="rel_bias_table",
    )(rel_bias_padded)


def _attn_kernel(qt_ref, kp_ref, kc_ref, vtp_ref, vtc_ref, tbl_ref, o_ref, kcat, vtcat, st_s, p_s):
    i = pl.program_id(2)
    kcat[0:PAD, :] = kp_ref[0]
    kcat[PAD:, :] = kc_ref[0]
    for hh in range(2):
        src = slice(hh * HEAD_DIM, (hh + 1) * HEAD_DIM)
        r0 = hh * VT_ROWS
        vtcat[r0:r0 + HEAD_DIM, 0:PAD] = vtp_ref[0, src, :]
        vtcat[r0:r0 + HEAD_DIM, PAD:] = vtc_ref[0, src, :]
        vtcat[r0 + HEAD_DIM:r0 + VT_ROWS, :] = jnp.ones((VT_ROWS - HEAD_DIM, PAD + ATT_TILE), BF16)

    feat = lax.broadcasted_iota(jnp.int32, (LANES, ATT_SUB), 0)
    first_head = feat < HEAD_DIM
    n_blk = ATT_TILE // ATT_SUB
    units = [(blk, hh) for blk in range(n_blk) for hh in range(2)]

    def scores(u):
        blk, hh = units[u]
        c0 = blk * ATT_SUB
        qt = qt_ref[0, :, c0:c0 + ATT_SUB]
        kwin = kcat[c0:c0 + ATT_WIN, :]
        variant = jnp.where(i == 0, blk + 1, 0) if blk + 1 < N_VARIANTS else 0
        mask = first_head if hh == 0 else jnp.logical_not(first_head)
        qm = jnp.where(mask, qt, jnp.zeros_like(qt))
        m = None
        for r0 in range(0, ATT_WIN, ATT_KEY_ROWS):
            rows = slice(r0, r0 + ATT_KEY_ROWS)
            st = _dot(kwin[rows], qm) + tbl_ref[variant, hh, rows, :]
            st_s[u % ATT_SLOTS, rows, :] = st
            piece_max = jnp.max(st, axis=0, keepdims=True)
            m = piece_max if m is None else jnp.maximum(m, piece_max)
        return m

    def softmax_pv(u, m):
        blk, hh = units[u]
        c0 = blk * ATT_SUB
        p_s[u % ATT_SLOTS] = jnp.exp2((st_s[u % ATT_SLOTS] - m).astype(BF16))
        vt = vtcat[hh * VT_ROWS:(hh + 1) * VT_ROWS, c0:c0 + ATT_WIN]
        ot = _dot(vt, p_s[u % ATT_SLOTS])
        return ot[:HEAD_DIM] * (1.0 / ot[HEAD_DIM:HEAD_DIM + 1])

    outs = []
    ms = [scores(u) for u in range(ATT_LOOKAHEAD)]
    for u in range(len(units)):
        if u + ATT_LOOKAHEAD < len(units):
            ms.append(scores(u + ATT_LOOKAHEAD))
        outs.append(softmax_pv(u, ms[u]))
        if u % 2 == 1:
            c0 = units[u][0] * ATT_SUB
            pair = jnp.concatenate(outs[-2:], axis=0)
            o_ref[0, c0:c0 + ATT_SUB, :] = pair.T.astype(BF16)


def _attention(qt, k, vt, table):
    bsz, s, _ = k.shape
    t = ATT_TILE
    prev_idx = lambda i: jnp.maximum(i * (t // PAD) - 1, 0)
    return pl.pallas_call(
        _attn_kernel,
        grid=(HEAD_PAIRS, bsz, s // t),
        in_specs=[
            pl.BlockSpec((1, LANES, t), lambda p, b, i: (b, p, i)),
            pl.BlockSpec((1, PAD, LANES), lambda p, b, i: (b, prev_idx(i), p)),
            pl.BlockSpec((1, t, LANES), lambda p, b, i: (b, i, p)),
            pl.BlockSpec((1, LANES, PAD), lambda p, b, i: (b, p, prev_idx(i))),
            pl.BlockSpec((1, LANES, t), lambda p, b, i: (b, p, i)),
            pl.BlockSpec((N_VARIANTS, 2, ATT_WIN, ATT_SUB), lambda p, b, i: (0, p, 0, 0)),
        ],
        out_specs=pl.BlockSpec((1, t, LANES), lambda p, b, i: (b, i, p)),
        out_shape=jax.ShapeDtypeStruct(k.shape, BF16),
        scratch_shapes=[pltpu.VMEM((PAD + t, LANES), BF16), pltpu.VMEM((2 * VT_ROWS, PAD + t), BF16),
                        pltpu.VMEM((ATT_SLOTS, ATT_WIN, ATT_SUB), F32),
                        pltpu.VMEM((ATT_SLOTS, ATT_WIN, ATT_SUB), BF16)],
        compiler_params=pltpu.CompilerParams(
            dimension_semantics=("arbitrary", "arbitrary", "arbitrary"), vmem_limit_bytes=VMEM_LIMIT),
        name="band_attention",
    )(qt, k, k, vt, vt, table)


def kernel(x, a_norm, a_w_in, a_conv_w, a_conv_b, a_w_gate, a_b_gate, a_lambda, a_w_out, kv_norm, w_kv, k_norm, b_norm, b_w_q, b_q_norm, b_rel_bias, b_w_o, mlp_norm, w_up, w_down):
    bsz, s, d = x.shape
    assert d == D_MODEL and s % ATT_TILE == 0 and s % (L0_SUBS * REC_TILE) == 0 and s % KVQ_TILE == 0
    assert a_norm.shape[0] == 1 and b_norm.shape[0] == 1 and mlp_norm.shape[0] == 2
    n = bsz * s
    row = lambda p: p.reshape(1, -1).astype(F32)

    h = _layer0(x.reshape(n, d), s // (L0_SUBS * REC_TILE), row(a_norm[0]), a_w_in[0].astype(BF16), a_conv_w[0],
                row(a_conv_b[0]), a_w_gate[0].astype(BF16), a_b_gate[0], row(a_lambda[0]),
                a_w_out[0].astype(BF16), row(mlp_norm[0]), w_up[0].astype(BF16), w_down[0].astype(BF16))

    col = lambda g: jnp.broadcast_to(jnp.tile(g, N_HEADS).astype(F32)[:, None], (D_MODEL, LANES))
    q_gain = col(b_q_norm[0]) * (HEAD_DIM ** -0.5 * LOG2E)
    k_gain = col(k_norm)
    shp = (bsz, s, d)
    qt, k, vt = _kvq(h.reshape(shp), row(kv_norm), row(b_norm[0]), w_kv[:, :D_MODEL].T.astype(BF16),
                     w_kv[:, D_MODEL:].T.astype(BF16), b_w_q[0].T.astype(BF16), k_gain, q_gain)
    table = _bias_table(jnp.pad(b_rel_bias[0], ((0, 0), (0, 2 * LANES - NREL))))
    attn = _attention(qt, k, vt, table)
    out = _proj_mlp(attn.reshape(n, d), b_w_o[0].astype(BF16), h, row(mlp_norm[1]),
                    w_up[1].astype(BF16), w_down[1].astype(BF16))
    return out.reshape(shp)
```

```python
import functools

import jax
import jax.numpy as jnp
from jax import lax
from jax.experimental import pallas as pl
from jax.experimental.pallas import tpu as pltpu

D_MODEL = 1024
D_RNN = D_MODEL
LRU_BLOCKS = 8
LRU_BW = D_RNN // LRU_BLOCKS
CONV_W = 4
LRU_C = 8.0
N_HEADS = 16
HEAD_DIM = 64
CHUNK = 64
LEFT_CHUNKS = 8
PAD = LEFT_CHUNKS * CHUNK
MAX_REL = 2 * CHUNK
MIN_REL = -(CHUNK - 1)
NREL = MAX_REL - MIN_REL + 1
D_FF = 4 * D_MODEL
EPS = 1e-6

LANES = 128
SUBLANES = 8
TINY = 1.1754944e-38
NEG = -1e30

REC_TILE = 256
L0_SUBS = 2
L0_FF_CHUNK = 1024
L0_ORDER = ("rec", "conv", "g0", "m", "g1", "gate", "m", "s0", "s1", "m", "s2", "s3", "gelu", "m", "out")
MLP_TILE = 1024
KVQ_TILE = 1024
KVQ_SUB = 512
ATT_TILE = 8192
ATT_SUB = 256
ATT_WIN = PAD + ATT_SUB
ATT_LOOKAHEAD = 4
ATT_SLOTS = ATT_LOOKAHEAD + 2
ATT_KEY_ROWS = 384
VT_ROWS = HEAD_DIM + 16
LOG2E = 1.4426950408889634
HEAD_PAIRS = N_HEADS // 2
VMEM_LIMIT = 56 * 1024 * 1024

F32 = jnp.float32
BF16 = jnp.bfloat16


def _dot(a, b):
    return jnp.dot(a, b, preferred_element_type=F32)


def _rms_scale(x):
    return lax.rsqrt(jnp.mean(x * x, axis=-1, keepdims=True) + EPS)


def _const_spec(shape):
    zeros = (0,) * len(shape)
    return pl.BlockSpec(shape, lambda *_: zeros, pipeline_mode=pl.Buffered(1))


def _layer0_kernel(x_ref, norm_ref, w_in_ref, cw_ref, cb_ref, wg_ref, bg_ref, lam_ref, w_out_ref,
                   mnorm_ref, w_up_ref, w_down_ref, o_ref,
                   rec_ext, a_s, b_s, h_carry, mix_s, *, tiles_per_seq):
    t = REC_TILE
    j = pl.program_id(0)

    @pl.when(j % tiles_per_seq == 0)
    def _():
        rec_ext[:, 0:SUBLANES, :] = jnp.zeros((D_RNN // LANES, SUBLANES, LANES), F32)
        h_carry[...] = jnp.zeros_like(h_carry)

    slot = j % 2
    row = lax.broadcasted_iota(jnp.int32, (SUBLANES, D_RNN), 0)
    n_slabs = t // SUBLANES
    last = pl.num_programs(0) - 1

    def run(mixer, mlp):
        h_state = h_carry[...] if mixer else None
        for q in range(L0_SUBS):
            h_state = _layer0_sub_tile(
                q, slot, h_state, row, n_slabs, x_ref, norm_ref, w_in_ref, cw_ref, cb_ref, wg_ref, bg_ref,
                lam_ref, w_out_ref, mnorm_ref, w_up_ref, w_down_ref, o_ref, rec_ext, a_s, b_s, mix_s,
                mixer=mixer, mlp=mlp)
        if mixer:
            h_carry[...] = h_state

    pl.when(j == 0)(functools.partial(run, True, False))
    pl.when((j > 0) & (j < last))(functools.partial(run, True, True))
    pl.when(j == last)(functools.partial(run, False, True))


def _layer0_sub_tile(q, slot, h_in, row, n_slabs, x_ref, norm_ref, w_in_ref, cw_ref, cb_ref, wg_ref,
                     bg_ref, lam_ref, w_out_ref, mnorm_ref, w_up_ref, w_down_ref, o_ref,
                     rec_ext, a_s, b_s, mix_s, *, mixer, mlp):
    t = REC_TILE
    rows_q = slice(q * t, (q + 1) * t)

    mlp_state = {"k": 0}
    if mlp:
        hp = mix_s[1 - slot, q]
        hn = (hp * _rms_scale(hp) * mnorm_ref[...]).astype(BF16)
        mlp_state["acc"] = hp

    def mlp_chunk():
        k = mlp_state["k"]
        sl = slice(k * L0_FF_CHUNK, (k + 1) * L0_FF_CHUNK)
        up = jnp.maximum(_dot(hn, w_up_ref[:, sl]), 0.0)
        mlp_state["acc"] = mlp_state["acc"] + _dot((up * up).astype(BF16), w_down_ref[sl, :])
        mlp_state["k"] = k + 1

    v = {"h": h_in}
    if mixer:
        x = x_ref[rows_q, :]
        xn = (x * _rms_scale(x) * norm_ref[...]).astype(BF16)

    def rec_proj():
        v["rec"] = _dot(xn, w_in_ref[:, D_RNN:])

    def gate_proj():
        v["gate"] = _dot(xn, w_in_ref[:, :D_RNN])

    def conv():
        cs = []
        for n in range(D_RNN // LANES):
            sl = slice(n * LANES, (n + 1) * LANES)
            rec = v["rec"][:, sl]
            rec_ext[n, SUBLANES:SUBLANES + t, :] = rec
            c = cb_ref[:, sl] + rec * cw_ref[CONV_W - 1:CONV_W, sl]
            for k in range(1, CONV_W):
                c = c + rec_ext[n, pl.ds(SUBLANES - k, t), :] * cw_ref[CONV_W - 1 - k:CONV_W - k, sl]
            cs.append(c)
            rec_ext[n, 0:SUBLANES, :] = rec_ext[n, t:t + SUBLANES, :]
        v["c"] = cs

    def gates(half):
        neg_lam = -lam_ref[...]
        softplus = jnp.maximum(neg_lam, 0.0) + jnp.log1p(jnp.exp(-jnp.abs(neg_lam)))
        for n in range(half * LRU_BLOCKS // 2, (half + 1) * LRU_BLOCKS // 2):
            sl = slice(n * LRU_BW, (n + 1) * LRU_BW)
            cb = v["c"][n]
            g = _dot(cb.astype(BF16), wg_ref[n]) + bg_ref[n:n + 1, :]
            r = jax.nn.sigmoid(g[:, :LRU_BW])
            i = jax.nn.sigmoid(g[:, LRU_BW:])
            log_a = (-LRU_C) * r * softplus[:, sl]
            a = jnp.exp(log_a)
            z = 1.0 - a * a
            mult = z * lax.rsqrt(jnp.maximum(z, TINY))
            a_s[q, :, sl] = a
            b_s[q, :, sl] = mult * (i * cb)

    def scan(quarter):
        for s in range(quarter * n_slabs // 4, (quarter + 1) * n_slabs // 4):
            rows = slice(s * SUBLANES, (s + 1) * SUBLANES)
            a = a_s[q, rows, :]
            b = b_s[q, rows, :]
            for d in (1, 2, 4):
                keep = row >= d
                a_sh = jnp.where(keep, pltpu.roll(a, d, 0), 1.0)
                b_sh = jnp.where(keep, pltpu.roll(b, d, 0), 0.0)
                b = a * b_sh + b
                a = a * a_sh
            h = a * v["h"] + b
            b_s[q, rows, :] = h
            v["h"] = h[SUBLANES - 1:SUBLANES, :]

    def gelu_gate():
        v["y"] = (jax.nn.gelu(v["gate"]) * b_s[q]).astype(BF16)

    def out_proj():
        mix_s[slot, q] = x + _dot(v["y"], w_out_ref[...])

    phases = {
        "rec": rec_proj, "gate": gate_proj, "conv": conv, "g0": lambda: gates(0), "g1": lambda: gates(1),
        "s0": lambda: scan(0), "s1": lambda: scan(1), "s2": lambda: scan(2), "s3": lambda: scan(3),
        "gelu": gelu_gate, "out": out_proj, "m": mlp_chunk,
    }
    for name in L0_ORDER:
        if mlp if name == "m" else mixer:
            phases[name]()
    if mlp:
        assert mlp_state["k"] == D_FF // L0_FF_CHUNK
        o_ref[rows_q, :] = mlp_state["acc"]
    return v["h"]


def _layer0(x2d, tiles_per_seq, norm, w_in, conv_w, conv_b, w_gate, b_gate, lam, w_out, mlp_norm, w_up, w_down):
    t = REC_TILE
    blk = L0_SUBS * t
    n_tiles = x2d.shape[0] // blk
    return pl.pallas_call(
        functools.partial(_layer0_kernel, tiles_per_seq=tiles_per_seq),
        grid=(n_tiles + 1,),
        in_specs=[
            pl.BlockSpec((blk, D_MODEL), lambda j: (jnp.minimum(j, n_tiles - 1), 0)),
            _const_spec((1, D_MODEL)),
            _const_spec((D_MODEL, 2 * D_RNN)),
            _const_spec((CONV_W, D_RNN)),
            _const_spec((1, D_RNN)),
            _const_spec((LRU_BLOCKS, LRU_BW, 2 * LRU_BW)),
            _const_spec((LRU_BLOCKS, 2 * LRU_BW)),
            _const_spec((1, D_RNN)),
            _const_spec((D_RNN, D_MODEL)),
            _const_spec((1, D_MODEL)),
            _const_spec((D_MODEL, D_FF)),
            _const_spec((D_FF, D_MODEL)),
        ],
        out_specs=pl.BlockSpec((blk, D_MODEL), lambda j: (jnp.maximum(j - 1, 0), 0)),
        out_shape=jax.ShapeDtypeStruct(x2d.shape, F32),
        scratch_shapes=[
            pltpu.VMEM((D_RNN // LANES, t + SUBLANES, LANES), F32),
            pltpu.VMEM((L0_SUBS, t, D_RNN), F32),
            pltpu.VMEM((L0_SUBS, t, D_RNN), F32),
            pltpu.VMEM((1, D_RNN), F32),
            pltpu.VMEM((2, L0_SUBS, t, D_MODEL), F32),
        ],
        compiler_params=pltpu.CompilerParams(
            dimension_semantics=("arbitrary",), vmem_limit_bytes=VMEM_LIMIT),
        name="layer0_mixer_mlp",
    )(x2d, norm, w_in, conv_w, conv_b, w_gate, b_gate, lam, w_out, mlp_norm, w_up, w_down)


FF_CHUNK = 1024


def _mlp_body(h, norm_ref, w_up_ref, w_down_ref):
    hn = (h * _rms_scale(h) * norm_ref[...]).astype(BF16)
    acc = h
    for j in range(D_FF // FF_CHUNK):
        sl = slice(j * FF_CHUNK, (j + 1) * FF_CHUNK)
        up = jnp.maximum(_dot(hn, w_up_ref[:, sl]), 0.0)
        acc = acc + _dot((up * up).astype(BF16), w_down_ref[sl, :])
    return acc


def _proj_mlp_kernel(a_ref, w_o_ref, x_ref, norm_ref, w_up_ref, w_down_ref, o_ref):
    h = x_ref[...] + _dot(a_ref[...], w_o_ref[...])
    o_ref[...] = _mlp_body(h, norm_ref, w_up_ref, w_down_ref)


def _proj_mlp(attn, w_o, x2d, norm, w_up, w_down):
    n = x2d.shape[0]
    t = MLP_TILE
    tile = pl.BlockSpec((t, D_MODEL), lambda i: (i, 0))
    return pl.pallas_call(
        _proj_mlp_kernel,
        grid=(n // t,),
        in_specs=[tile, _const_spec((D_MODEL, D_MODEL)), tile, _const_spec((1, D_MODEL)),
                  _const_spec((D_MODEL, D_FF)), _const_spec((D_FF, D_MODEL))],
        out_specs=tile,
        out_shape=jax.ShapeDtypeStruct(x2d.shape, F32),
        compiler_params=pltpu.CompilerParams(
            dimension_semantics=("arbitrary",), vmem_limit_bytes=VMEM_LIMIT),
        name="proj_mlp",
    )(attn, w_o, x2d, norm, w_up, w_down)


def _dot_nt(a, b):
    return lax.dot_general(a, b, (((1,), (1,)), ((), ())), preferred_element_type=F32)


def _head_rms_normalise_t(zt, gain_ref):
    t = zt.shape[1]
    z3 = zt.reshape(N_HEADS, HEAD_DIM, t)
    inv = lax.rsqrt(jnp.mean(z3 * z3, axis=1, keepdims=True) + EPS)
    gain = jnp.concatenate([gain_ref[...]] * (t // LANES), axis=1)
    return (z3 * inv).reshape(N_HEADS * HEAD_DIM, t) * gain


def _kvq_kernel(x_ref, kvn_ref, qn_ref, w_kt_ref, w_vt_ref, w_qt_ref, kg_ref, qg_ref,
                qt_ref, k_ref, vt_ref):
    def project(r0):
        x = x_ref[0, r0:r0 + KVQ_SUB, :]
        xs = x * _rms_scale(x)
        xkv = (xs * kvn_ref[...]).astype(BF16)
        xq = (xs * qn_ref[...]).astype(BF16)
        return _dot_nt(w_vt_ref[...], xkv), _dot_nt(w_kt_ref[...], xkv), _dot_nt(w_qt_ref[...], xq)

    def finish(r0, vt, kt, qt):
        rows = slice(r0, r0 + KVQ_SUB)
        vt_ref[0, :, rows] = vt.astype(BF16)
        k_ref[0, rows, :] = _head_rms_normalise_t(kt, kg_ref).T.astype(BF16)
        qt_ref[0, :, rows] = _head_rms_normalise_t(qt, qg_ref).astype(BF16)

    starts = list(range(0, KVQ_TILE, KVQ_SUB))
    pending = project(starts[0])
    for n, r0 in enumerate(starts):
        nxt = project(starts[n + 1]) if n + 1 < len(starts) else None
        finish(r0, *pending)
        pending = nxt


def _kvq(x, kv_norm, q_norm, w_kt, w_vt, w_qt, k_gain, q_gain):
    bsz, s, _ = x.shape
    t = KVQ_TILE
    tok_major = pl.BlockSpec((1, t, D_MODEL), lambda b, i: (b, i, 0))
    feat_major = pl.BlockSpec((1, D_MODEL, t), lambda b, i: (b, 0, i))
    w_spec = _const_spec((D_MODEL, D_MODEL))
    g_spec = _const_spec((D_MODEL, LANES))
    ft = jax.ShapeDtypeStruct((bsz, D_MODEL, s), BF16)
    return pl.pallas_call(
        _kvq_kernel,
        grid=(bsz, s // t),
        in_specs=[tok_major, _const_spec((1, D_MODEL)), _const_spec((1, D_MODEL)),
                  w_spec, w_spec, w_spec, g_spec, g_spec],
        out_specs=[feat_major, tok_major, feat_major],
        out_shape=[ft, jax.ShapeDtypeStruct(x.shape, BF16), ft],
        compiler_params=pltpu.CompilerParams(
            dimension_semantics=("arbitrary", "arbitrary"), vmem_limit_bytes=VMEM_LIMIT),
        name="kvq_proj",
    )(x, kv_norm, q_norm, w_kt, w_vt, w_qt, k_gain, q_gain)


BASE_W = ATT_WIN + ATT_SUB
N_VARIANTS = 1 + PAD // ATT_SUB
ROLL_ROWS = 128


def _bias_kernel(rb_ref, o_ref, base_s):
    rb = rb_ref[...] * LOG2E
    r_idx = lax.broadcasted_iota(jnp.int32, (2 * LANES, BASE_W), 0)
    m_idx = lax.broadcasted_iota(jnp.int32, (2 * LANES, BASE_W), 1)
    want = jnp.clip(m_idx - ATT_SUB, MIN_REL, MAX_REL) - MIN_REL
    onehot = (r_idx == want).astype(BF16)
    p0 = rb.astype(BF16)
    r1 = rb - p0.astype(F32)
    p1 = r1.astype(BF16)
    p2 = (r1 - p1.astype(F32)).astype(BF16)
    base_s[...] = _dot(p0, onehot) + _dot(p1, onehot) + _dot(p2, onehot)
    base = base_s[pl.ds(pl.program_id(0), 1), :]

    for r0 in range(0, ATT_WIN, ROLL_ROWS):
        kj = r0 + lax.broadcasted_iota(jnp.int32, (ROLL_ROWS, ATT_SUB), 0)
        qi = lax.broadcasted_iota(jnp.int32, (ROLL_ROWS, ATT_SUB), 1)
        band_lo = (qi // CHUNK) * CHUNK
        in_band = (kj >= band_lo) & (kj < band_lo + PAD + CHUNK)
        rows = jnp.broadcast_to(base, (ROLL_ROWS, BASE_W))
        toeplitz = pltpu.roll(rows, r0, 1, stride=1, stride_axis=0)
        o_ref[0, r0:r0 + ROLL_ROWS, :] = jnp.where(in_band, toeplitz[:, ATT_WIN:], NEG)


def _bias_table(rel_bias_padded):
    return pl.pallas_call(
        _bias_kernel,
        grid=(N_HEADS,),
        in_specs=[_const_spec((N_HEADS, 2 * LANES))],
        out_specs=pl.BlockSpec((1, ATT_WIN, ATT_SUB), lambda h: (h, 0, 0)),
        out_shape=jax.ShapeDtypeStruct((N_HEADS, ATT_WIN, ATT_SUB), F32),
        scratch_shapes=[pltpu.VMEM((N_HEADS, BASE_W), F32)],
        compiler_params=pltpu.CompilerParams(
            dimension_semantics=("arbitrary",), vmem_limit_bytes=VMEM_LIMIT),
        name="rel_bias_table",
    )(rel_bias_padded)


def _attn_kernel(qt_ref, kp_ref, kc_ref, vtp_ref, vtc_ref, tbl_in_ref, o_ref, kcat, vtcat, st_s, p_s, tbl_ref):
    i = pl.program_id(2)

    @pl.when((pl.program_id(1) == 0) & (i == 0))
    def _():
        kj = lax.broadcasted_iota(jnp.int32, (ATT_WIN, ATT_SUB), 0)
        for hh in range(2):
            base = tbl_in_ref[hh]
            tbl_ref[0, hh] = base
            for v in range(1, N_VARIANTS):
                tbl_ref[v, hh] = jnp.where(kj >= PAD - (v - 1) * ATT_SUB, base, NEG)

    kcat[0:PAD, :] = kp_ref[0]
    kcat[PAD:, :] = kc_ref[0]
    for hh in range(2):
        src = slice(hh * HEAD_DIM, (hh + 1) * HEAD_DIM)
        r0 = hh * VT_ROWS
        vtcat[r0:r0 + HEAD_DIM, 0:PAD] = vtp_ref[0, src, :]
        vtcat[r0:r0 + HEAD_DIM, PAD:] = vtc_ref[0, src, :]
        vtcat[r0 + HEAD_DIM:r0 + VT_ROWS, :] = jnp.ones((VT_ROWS - HEAD_DIM, PAD + ATT_TILE), BF16)

    feat = lax.broadcasted_iota(jnp.int32, (LANES, ATT_SUB), 0)
    first_head = feat < HEAD_DIM
    n_blk = ATT_TILE // ATT_SUB
    units = [(blk, hh) for blk in range(n_blk) for hh in range(2)]

    def scores(u):
        blk, hh = units[u]
        c0 = blk * ATT_SUB
        qt = qt_ref[0, :, c0:c0 + ATT_SUB]
        kwin = kcat[c0:c0 + ATT_WIN, :]
        variant = jnp.where(i == 0, blk + 1, 0) if blk + 1 < N_VARIANTS else 0
        mask = first_head if hh == 0 else jnp.logical_not(first_head)
        qm = jnp.where(mask, qt, jnp.zeros_like(qt))
        m = None
        for r0 in range(0, ATT_WIN, ATT_KEY_ROWS):
            rows = slice(r0, r0 + ATT_KEY_ROWS)
            st = _dot(kwin[rows], qm) + tbl_ref[variant, hh, rows, :]
            st_s[u % ATT_SLOTS, rows, :] = st
            piece_max = jnp.max(st, axis=0, keepdims=True)
            m = piece_max if m is None else jnp.maximum(m, piece_max)
        return m

    def softmax_pv(u, m):
        blk, hh = units[u]
        c0 = blk * ATT_SUB
        p_s[u % ATT_SLOTS] = jnp.exp2((st_s[u % ATT_SLOTS] - m).astype(BF16))
        vt = vtcat[hh * VT_ROWS:(hh + 1) * VT_ROWS, c0:c0 + ATT_WIN]
        ot = _dot(vt, p_s[u % ATT_SLOTS])
        return ot[:HEAD_DIM] * (1.0 / ot[HEAD_DIM:HEAD_DIM + 1])

    outs = []
    ms = [scores(u) for u in range(ATT_LOOKAHEAD)]
    for u in range(len(units)):
        if u + ATT_LOOKAHEAD < len(units):
            ms.append(scores(u + ATT_LOOKAHEAD))
        outs.append(softmax_pv(u, ms[u]))
        if u % 2 == 1:
            c0 = units[u][0] * ATT_SUB
            pair = jnp.concatenate(outs[-2:], axis=0)
            o_ref[0, c0:c0 + ATT_SUB, :] = pair.T.astype(BF16)


def _attention(qt, k, vt, table):
    bsz, s, _ = k.shape
    t = ATT_TILE
    prev_idx = lambda i: jnp.maximum(i * (t // PAD) - 1, 0)
    return pl.pallas_call(
        _attn_kernel,
        grid=(HEAD_PAIRS, bsz, s // t),
        in_specs=[
            pl.BlockSpec((1, LANES, t), lambda p, b, i: (b, p, i)),
            pl.BlockSpec((1, PAD, LANES), lambda p, b, i: (b, prev_idx(i), p)),
            pl.BlockSpec((1, t, LANES), lambda p, b, i: (b, i, p)),
            pl.BlockSpec((1, LANES, PAD), lambda p, b, i: (b, p, prev_idx(i))),
            pl.BlockSpec((1, LANES, t), lambda p, b, i: (b, p, i)),
            pl.BlockSpec((2, ATT_WIN, ATT_SUB), lambda p, b, i: (p, 0, 0)),
        ],
        out_specs=pl.BlockSpec((1, t, LANES), lambda p, b, i: (b, i, p)),
        out_shape=jax.ShapeDtypeStruct(k.shape, BF16),
        scratch_shapes=[pltpu.VMEM((PAD + t, LANES), BF16), pltpu.VMEM((2 * VT_ROWS, PAD + t), BF16),
                        pltpu.VMEM((ATT_SLOTS, ATT_WIN, ATT_SUB), F32),
                        pltpu.VMEM((ATT_SLOTS, ATT_WIN, ATT_SUB), BF16),
                        pltpu.VMEM((N_VARIANTS, 2, ATT_WIN, ATT_SUB), F32)],
        compiler_params=pltpu.CompilerParams(
            dimension_semantics=("arbitrary", "arbitrary", "arbitrary"), vmem_limit_bytes=VMEM_LIMIT),
        name="band_attention",
    )(qt, k, k, vt, vt, table)


def kernel(x, a_norm, a_w_in, a_conv_w, a_conv_b, a_w_gate, a_b_gate, a_lambda, a_w_out, kv_norm, w_kv, k_norm, b_norm, b_w_q, b_q_norm, b_rel_bias, b_w_o, mlp_norm, w_up, w_down):
    bsz, s, d = x.shape
    assert d == D_MODEL and s % ATT_TILE == 0 and s % (L0_SUBS * REC_TILE) == 0 and s % KVQ_TILE == 0
    assert a_norm.shape[0] == 1 and b_norm.shape[0] == 1 and mlp_norm.shape[0] == 2
    n = bsz * s
    row = lambda p: p.reshape(1, -1).astype(F32)

    h = _layer0(x.reshape(n, d), s // (L0_SUBS * REC_TILE), row(a_norm[0]), a_w_in[0].astype(BF16), a_conv_w[0],
                row(a_conv_b[0]), a_w_gate[0].astype(BF16), a_b_gate[0], row(a_lambda[0]),
                a_w_out[0].astype(BF16), row(mlp_norm[0]), w_up[0].astype(BF16), w_down[0].astype(BF16))

    col = lambda g: jnp.broadcast_to(jnp.tile(g, N_HEADS).astype(F32)[:, None], (D_MODEL, LANES))
    q_gain = col(b_q_norm[0]) * (HEAD_DIM ** -0.5 * LOG2E)
    k_gain = col(k_norm)
    shp = (bsz, s, d)
    qt, k, vt = _kvq(h.reshape(shp), row(kv_norm), row(b_norm[0]), w_kv[:, :D_MODEL].T.astype(BF16),
                     w_kv[:, D_MODEL:].T.astype(BF16), b_w_q[0].T.astype(BF16), k_gain, q_gain)
    table = _bias_table(jnp.pad(b_rel_bias[0], ((0, 0), (0, 2 * LANES - NREL))))
    attn = _attention(qt, k, vt, table)
    out = _proj_mlp(attn.reshape(n, d), b_w_o[0].astype(BF16), h, row(mlp_norm[1]),
                    w_up[1].astype(BF16), w_down[1].astype(BF16))
    return out.reshape(shp)
```

```python
import functools

import jax
import jax.numpy as jnp
from jax import lax
from jax.experimental import pallas as pl
from jax.experimental.pallas import tpu as pltpu

D_MODEL = 1024
D_RNN = D_MODEL
LRU_BLOCKS = 8
LRU_BW = D_RNN // LRU_BLOCKS
CONV_W = 4
LRU_C = 8.0
N_HEADS = 16
HEAD_DIM = 64
CHUNK = 64
LEFT_CHUNKS = 8
PAD = LEFT_CHUNKS * CHUNK
MAX_REL = 2 * CHUNK
MIN_REL = -(CHUNK - 1)
NREL = MAX_REL - MIN_REL + 1
D_FF = 4 * D_MODEL
EPS = 1e-6

LANES = 128
SUBLANES = 8
TINY = 1.1754944e-38
NEG = -1e30

REC_TILE = 256
L0_SUBS = 2
L0_FF_CHUNK = 1024
L0_ORDER = ("rec", "conv", "g0", "m", "g1", "gate", "m", "s0", "s1", "m", "s2", "s3", "gelu", "m", "out")
MLP_TILE = 1024
KVQ_TILE = 1024
KVQ_SUB = 512
ATT_TILE = 8192
ATT_SUB = 256
ATT_WIN = PAD + ATT_SUB
ATT_LOOKAHEAD = 4
ATT_SLOTS = ATT_LOOKAHEAD + 2
ATT_KEY_ROWS = 384
VT_ROWS = HEAD_DIM + 16
LOG2E = 1.4426950408889634
HEAD_PAIRS = N_HEADS // 2
VMEM_LIMIT = 56 * 1024 * 1024

F32 = jnp.float32
BF16 = jnp.bfloat16


def _dot(a, b):
    return jnp.dot(a, b, preferred_element_type=F32)


def _rms_scale(x):
    return lax.rsqrt(jnp.mean(x * x, axis=-1, keepdims=True) + EPS)


def _const_spec(shape):
    zeros = (0,) * len(shape)
    return pl.BlockSpec(shape, lambda *_: zeros, pipeline_mode=pl.Buffered(1))


def _layer0_kernel(x_ref, norm_ref, w_in_ref, cw_ref, cb_ref, wg_ref, bg_ref, lam_ref, w_out_ref,
                   mnorm_ref, w_up_ref, w_down_ref, o_ref,
                   rec_ext, a_s, b_s, h_carry, mix_s, *, tiles_per_seq):
    t = REC_TILE
    j = pl.program_id(0)

    @pl.when(j % tiles_per_seq == 0)
    def _():
        rec_ext[:, 0:SUBLANES, :] = jnp.zeros((D_RNN // LANES, SUBLANES, LANES), F32)
        h_carry[...] = jnp.zeros_like(h_carry)

    @pl.when(j == 0)
    def _():
        mix_s[1] = jnp.zeros((L0_SUBS, t, D_MODEL), F32)

    slot = j % 2
    row = lax.broadcasted_iota(jnp.int32, (SUBLANES, D_RNN), 0)
    n_slabs = t // SUBLANES
    h_state = h_carry[...]
    for q in range(L0_SUBS):
        h_state = _layer0_sub_tile(
            q, slot, h_state, row, n_slabs, x_ref, norm_ref, w_in_ref, cw_ref, cb_ref, wg_ref, bg_ref,
            lam_ref, w_out_ref, mnorm_ref, w_up_ref, w_down_ref, o_ref, rec_ext, a_s, b_s, mix_s)
    h_carry[...] = h_state


def _layer0_sub_tile(q, slot, h_in, row, n_slabs, x_ref, norm_ref, w_in_ref, cw_ref, cb_ref, wg_ref,
                     bg_ref, lam_ref, w_out_ref, mnorm_ref, w_up_ref, w_down_ref, o_ref,
                     rec_ext, a_s, b_s, mix_s):
    t = REC_TILE
    rows_q = slice(q * t, (q + 1) * t)

    hp = mix_s[1 - slot, q]
    hn = (hp * _rms_scale(hp) * mnorm_ref[...]).astype(BF16)

    mlp_state = {"acc": hp, "k": 0}

    def mlp_chunk():
        k = mlp_state["k"]
        sl = slice(k * L0_FF_CHUNK, (k + 1) * L0_FF_CHUNK)
        up = jnp.maximum(_dot(hn, w_up_ref[:, sl]), 0.0)
        mlp_state["acc"] = mlp_state["acc"] + _dot((up * up).astype(BF16), w_down_ref[sl, :])
        mlp_state["k"] = k + 1

    x = x_ref[rows_q, :]
    xn = (x * _rms_scale(x) * norm_ref[...]).astype(BF16)
    v = {"h": h_in}

    def rec_proj():
        v["rec"] = _dot(xn, w_in_ref[:, D_RNN:])

    def gate_proj():
        v["gate"] = _dot(xn, w_in_ref[:, :D_RNN])

    def conv():
        cs = []
        for n in range(D_RNN // LANES):
            sl = slice(n * LANES, (n + 1) * LANES)
            rec = v["rec"][:, sl]
            rec_ext[n, SUBLANES:SUBLANES + t, :] = rec
            c = cb_ref[:, sl] + rec * cw_ref[CONV_W - 1:CONV_W, sl]
            for k in range(1, CONV_W):
                c = c + rec_ext[n, pl.ds(SUBLANES - k, t), :] * cw_ref[CONV_W - 1 - k:CONV_W - k, sl]
            cs.append(c)
            rec_ext[n, 0:SUBLANES, :] = rec_ext[n, t:t + SUBLANES, :]
        v["c"] = cs

    def gates(half):
        neg_lam = -lam_ref[...]
        softplus = jnp.maximum(neg_lam, 0.0) + jnp.log1p(jnp.exp(-jnp.abs(neg_lam)))
        for n in range(half * LRU_BLOCKS // 2, (half + 1) * LRU_BLOCKS // 2):
            sl = slice(n * LRU_BW, (n + 1) * LRU_BW)
            cb = v["c"][n]
            g = _dot(cb.astype(BF16), wg_ref[n]) + bg_ref[n:n + 1, :]
            r = jax.nn.sigmoid(g[:, :LRU_BW])
            i = jax.nn.sigmoid(g[:, LRU_BW:])
            log_a = (-LRU_C) * r * softplus[:, sl]
            a = jnp.exp(log_a)
            z = 1.0 - a * a
            mult = z * lax.rsqrt(jnp.maximum(z, TINY))
            a_s[q, :, sl] = a
            b_s[q, :, sl] = mult * (i * cb)

    def scan(quarter):
        for s in range(quarter * n_slabs // 4, (quarter + 1) * n_slabs // 4):
            rows = slice(s * SUBLANES, (s + 1) * SUBLANES)
            a = a_s[q, rows, :]
            b = b_s[q, rows, :]
            for d in (1, 2, 4):
                keep = row >= d
                a_sh = jnp.where(keep, pltpu.roll(a, d, 0), 1.0)
                b_sh = jnp.where(keep, pltpu.roll(b, d, 0), 0.0)
                b = a * b_sh + b
                a = a * a_sh
            h = a * v["h"] + b
            b_s[q, rows, :] = h
            v["h"] = h[SUBLANES - 1:SUBLANES, :]

    def gelu_gate():
        v["y"] = (jax.nn.gelu(v["gate"]) * b_s[q]).astype(BF16)

    def out_proj():
        mix_s[slot, q] = x + _dot(v["y"], w_out_ref[...])

    phases = {
        "rec": rec_proj, "gate": gate_proj, "conv": conv, "g0": lambda: gates(0), "g1": lambda: gates(1),
        "s0": lambda: scan(0), "s1": lambda: scan(1), "s2": lambda: scan(2), "s3": lambda: scan(3),
        "gelu": gelu_gate, "out": out_proj, "m": mlp_chunk,
    }
    for name in L0_ORDER:
        phases[name]()
    assert mlp_state["k"] == D_FF // L0_FF_CHUNK
    o_ref[rows_q, :] = mlp_state["acc"]
    return v["h"]


def _layer0(x2d, tiles_per_seq, norm, w_in, conv_w, conv_b, w_gate, b_gate, lam, w_out, mlp_norm, w_up, w_down):
    t = REC_TILE
    blk = L0_SUBS * t
    n_tiles = x2d.shape[0] // blk
    return pl.pallas_call(
        functools.partial(_layer0_kernel, tiles_per_seq=tiles_per_seq),
        grid=(n_tiles + 1,),
        in_specs=[
            pl.BlockSpec((blk, D_MODEL), lambda j: (jnp.minimum(j, n_tiles - 1), 0)),
            _const_spec((1, D_MODEL)),
            _const_spec((D_MODEL, 2 * D_RNN)),
            _const_spec((CONV_W, D_RNN)),
            _const_spec((1, D_RNN)),
            _const_spec((LRU_BLOCKS, LRU_BW, 2 * LRU_BW)),
            _const_spec((LRU_BLOCKS, 2 * LRU_BW)),
            _const_spec((1, D_RNN)),
            _const_spec((D_RNN, D_MODEL)),
            _const_spec((1, D_MODEL)),
            _const_spec((D_MODEL, D_FF)),
            _const_spec((D_FF, D_MODEL)),
        ],
        out_specs=pl.BlockSpec((blk, D_MODEL), lambda j: (jnp.maximum(j - 1, 0), 0)),
        out_shape=jax.ShapeDtypeStruct(x2d.shape, F32),
        scratch_shapes=[
            pltpu.VMEM((D_RNN // LANES, t + SUBLANES, LANES), F32),
            pltpu.VMEM((L0_SUBS, t, D_RNN), F32),
            pltpu.VMEM((L0_SUBS, t, D_RNN), F32),
            pltpu.VMEM((1, D_RNN), F32),
            pltpu.VMEM((2, L0_SUBS, t, D_MODEL), F32),
        ],
        compiler_params=pltpu.CompilerParams(
            dimension_semantics=("arbitrary",), vmem_limit_bytes=VMEM_LIMIT),
        name="layer0_mixer_mlp",
    )(x2d, norm, w_in, conv_w, conv_b, w_gate, b_gate, lam, w_out, mlp_norm, w_up, w_down)


FF_CHUNK = 1024


def _mlp_body(h, norm_ref, w_up_ref, w_down_ref):
    hn = (h * _rms_scale(h) * norm_ref[...]).astype(BF16)
    acc = h
    for j in range(D_FF // FF_CHUNK):
        sl = slice(j * FF_CHUNK, (j + 1) * FF_CHUNK)
        up = jnp.maximum(_dot(hn, w_up_ref[:, sl]), 0.0)
        acc = acc + _dot((up * up).astype(BF16), w_down_ref[sl, :])
    return acc


def _proj_mlp_kernel(a_ref, w_o_ref, x_ref, norm_ref, w_up_ref, w_down_ref, o_ref):
    attn = jnp.concatenate([a_ref[0, p] for p in range(HEAD_PAIRS)], axis=1)
    h = x_ref[...] + _dot(attn, w_o_ref[...])
    o_ref[...] = _mlp_body(h, norm_ref, w_up_ref, w_down_ref)


def _proj_mlp(attn, w_o, x2d, norm, w_up, w_down):
    n = x2d.shape[0]
    t = MLP_TILE
    tiles_per_seq = attn.shape[2] // t
    tile = pl.BlockSpec((t, D_MODEL), lambda i: (i, 0))
    attn_tile = pl.BlockSpec((1, HEAD_PAIRS, t, LANES), lambda i: (i // tiles_per_seq, 0, i % tiles_per_seq, 0))
    return pl.pallas_call(
        _proj_mlp_kernel,
        grid=(n // t,),
        in_specs=[attn_tile, _const_spec((D_MODEL, D_MODEL)), tile, _const_spec((1, D_MODEL)),
                  _const_spec((D_MODEL, D_FF)), _const_spec((D_FF, D_MODEL))],
        out_specs=tile,
        out_shape=jax.ShapeDtypeStruct(x2d.shape, F32),
        compiler_params=pltpu.CompilerParams(
            dimension_semantics=("arbitrary",), vmem_limit_bytes=VMEM_LIMIT),
        name="proj_mlp",
    )(attn, w_o, x2d, norm, w_up, w_down)


def _dot_nt(a, b):
    return lax.dot_general(a, b, (((1,), (1,)), ((), ())), preferred_element_type=F32)


def _head_rms_normalise_t(zt, gain_ref):
    t = zt.shape[1]
    z3 = zt.reshape(N_HEADS, HEAD_DIM, t)
    inv = lax.rsqrt(jnp.mean(z3 * z3, axis=1, keepdims=True) + EPS)
    gain = jnp.concatenate([gain_ref[...]] * (t // LANES), axis=1)
    return (z3 * inv).reshape(N_HEADS * HEAD_DIM, t) * gain


def _kvq_kernel(x_ref, kvn_ref, qn_ref, w_kt_ref, w_vt_ref, w_qt_ref, kg_ref, qg_ref,
                qt_ref, k_ref, vt_ref):
    def project(r0):
        x = x_ref[0, r0:r0 + KVQ_SUB, :]
        xs = x * _rms_scale(x)
        xkv = (xs * kvn_ref[...]).astype(BF16)
        xq = (xs * qn_ref[...]).astype(BF16)
        return _dot_nt(w_vt_ref[...], xkv), _dot_nt(w_kt_ref[...], xkv), _dot_nt(w_qt_ref[...], xq)

    def finish(r0, vt, kt, qt):
        rows = slice(r0, r0 + KVQ_SUB)
        vt_ref[0, :, rows] = vt.astype(BF16)
        kn = _head_rms_normalise_t(kt, kg_ref).T.astype(BF16)
        for p in range(HEAD_PAIRS):
            k_ref[0, p, rows, :] = kn[:, p * LANES:(p + 1) * LANES]
        qt_ref[0, :, rows] = _head_rms_normalise_t(qt, qg_ref).astype(BF16)

    starts = list(range(0, KVQ_TILE, KVQ_SUB))
    pending = project(starts[0])
    for n, r0 in enumerate(starts):
        nxt = project(starts[n + 1]) if n + 1 < len(starts) else None
        finish(r0, *pending)
        pending = nxt


def _kvq(x, kv_norm, q_norm, w_kt, w_vt, w_qt, k_gain, q_gain):
    bsz, s, _ = x.shape
    t = KVQ_TILE
    tok_major = pl.BlockSpec((1, t, D_MODEL), lambda b, i: (b, i, 0))
    feat_major = pl.BlockSpec((1, D_MODEL, t), lambda b, i: (b, 0, i))
    w_spec = _const_spec((D_MODEL, D_MODEL))
    g_spec = _const_spec((D_MODEL, LANES))
    ft = jax.ShapeDtypeStruct((bsz, D_MODEL, s), BF16)
    return pl.pallas_call(
        _kvq_kernel,
        grid=(bsz, s // t),
        in_specs=[tok_major, _const_spec((1, D_MODEL)), _const_spec((1, D_MODEL)),
                  w_spec, w_spec, w_spec, g_spec, g_spec],
        out_specs=[feat_major, pl.BlockSpec((1, HEAD_PAIRS, t, LANES), lambda b, i: (b, 0, i, 0)), feat_major],
        out_shape=[ft, jax.ShapeDtypeStruct((bsz, HEAD_PAIRS, s, LANES), BF16), ft],
        compiler_params=pltpu.CompilerParams(
            dimension_semantics=("arbitrary", "arbitrary"), vmem_limit_bytes=VMEM_LIMIT),
        name="kvq_proj",
    )(x, kv_norm, q_norm, w_kt, w_vt, w_qt, k_gain, q_gain)


BASE_W = ATT_WIN + ATT_SUB
N_VARIANTS = 1 + PAD // ATT_SUB
ROLL_ROWS = 128


def _bias_kernel(rb_ref, o_ref, base_s):
    rb = rb_ref[...] * LOG2E
    r_idx = lax.broadcasted_iota(jnp.int32, (2 * LANES, BASE_W), 0)
    m_idx = lax.broadcasted_iota(jnp.int32, (2 * LANES, BASE_W), 1)
    want = jnp.clip(m_idx - ATT_SUB, MIN_REL, MAX_REL) - MIN_REL
    onehot = (r_idx == want).astype(BF16)
    p0 = rb.astype(BF16)
    r1 = rb - p0.astype(F32)
    p1 = r1.astype(BF16)
    p2 = (r1 - p1.astype(F32)).astype(BF16)
    base_s[...] = _dot(p0, onehot) + _dot(p1, onehot) + _dot(p2, onehot)
    base = base_s[pl.ds(pl.program_id(0), 1), :]

    for r0 in range(0, ATT_WIN, ROLL_ROWS):
        kj = r0 + lax.broadcasted_iota(jnp.int32, (ROLL_ROWS, ATT_SUB), 0)
        qi = lax.broadcasted_iota(jnp.int32, (ROLL_ROWS, ATT_SUB), 1)
        band_lo = (qi // CHUNK) * CHUNK
        in_band = (kj >= band_lo) & (kj < band_lo + PAD + CHUNK)
        rows = jnp.broadcast_to(base, (ROLL_ROWS, BASE_W))
        toeplitz = pltpu.roll(rows, r0, 1, stride=1, stride_axis=0)
        tile = jnp.where(in_band, toeplitz[:, ATT_WIN:], NEG)
        o_ref[0, 0, r0:r0 + ROLL_ROWS, :] = tile
        for v in range(1, N_VARIANTS):
            o_ref[v, 0, r0:r0 + ROLL_ROWS, :] = jnp.where(kj >= PAD - (v - 1) * ATT_SUB, tile, NEG)


def _bias_table(rel_bias_padded):
    return pl.pallas_call(
        _bias_kernel,
        grid=(N_HEADS,),
        in_specs=[_const_spec((N_HEADS, 2 * LANES))],
        out_specs=pl.BlockSpec((N_VARIANTS, 1, ATT_WIN, ATT_SUB), lambda h: (0, h, 0, 0)),
        out_shape=jax.ShapeDtypeStruct((N_VARIANTS, N_HEADS, ATT_WIN, ATT_SUB), F32),
        scratch_shapes=[pltpu.VMEM((N_HEADS, BASE_W), F32)],
        compiler_params=pltpu.CompilerParams(
            dimension_semantics=("arbitrary",), vmem_limit_bytes=VMEM_LIMIT),
        name="rel_bias_table",
    )(rel_bias_padded)


def _attn_kernel(qt_ref, kp_ref, kc_ref, vtp_ref, vtc_ref, tbl_ref, o_ref, kcat, vtcat, st_s, p_s):
    i = pl.program_id(2)
    kcat[0:PAD, :] = kp_ref[0, 0]
    kcat[PAD:, :] = kc_ref[0, 0]
    for hh in range(2):
        src = slice(hh * HEAD_DIM, (hh + 1) * HEAD_DIM)
        r0 = hh * VT_ROWS
        vtcat[r0:r0 + HEAD_DIM, 0:PAD] = vtp_ref[0, src, :]
        vtcat[r0:r0 + HEAD_DIM, PAD:] = vtc_ref[0, src, :]
        vtcat[r0 + HEAD_DIM:r0 + VT_ROWS, :] = jnp.ones((VT_ROWS - HEAD_DIM, PAD + ATT_TILE), BF16)

    feat = lax.broadcasted_iota(jnp.int32, (LANES, ATT_SUB), 0)
    first_head = feat < HEAD_DIM
    n_blk = ATT_TILE // ATT_SUB
    units = [(blk, hh) for blk in range(n_blk) for hh in range(2)]

    def scores(u):
        blk, hh = units[u]
        c0 = blk * ATT_SUB
        qt = qt_ref[0, :, c0:c0 + ATT_SUB]
        kwin = kcat[c0:c0 + ATT_WIN, :]
        variant = jnp.where(i == 0, blk + 1, 0) if blk + 1 < N_VARIANTS else 0
        mask = first_head if hh == 0 else jnp.logical_not(first_head)
        qm = jnp.where(mask, qt, jnp.zeros_like(qt))
        m = None
        for r0 in range(0, ATT_WIN, ATT_KEY_ROWS):
            rows = slice(r0, r0 + ATT_KEY_ROWS)
            st = _dot(kwin[rows], qm) + tbl_ref[variant, hh, rows, :]
            st_s[u % ATT_SLOTS, rows, :] = st
            piece_max = jnp.max(st, axis=0, keepdims=True)
            m = piece_max if m is None else jnp.maximum(m, piece_max)
        return m

    def softmax_pv(u, m):
        blk, hh = units[u]
        c0 = blk * ATT_SUB
        p_s[u % ATT_SLOTS] = jnp.exp2((st_s[u % ATT_SLOTS] - m).astype(BF16))
        vt = vtcat[hh * VT_ROWS:(hh + 1) * VT_ROWS, c0:c0 + ATT_WIN]
        ot = _dot(vt, p_s[u % ATT_SLOTS])
        return ot[:HEAD_DIM] * (1.0 / ot[HEAD_DIM:HEAD_DIM + 1])

    outs = []
    ms = [scores(u) for u in range(ATT_LOOKAHEAD)]
    for u in range(len(units)):
        if u + ATT_LOOKAHEAD < len(units):
            ms.append(scores(u + ATT_LOOKAHEAD))
        outs.append(softmax_pv(u, ms[u]))
        if u % 2 == 1:
            c0 = units[u][0] * ATT_SUB
            pair = jnp.concatenate(outs[-2:], axis=0)
            o_ref[0, 0, c0:c0 + ATT_SUB, :] = pair.T.astype(BF16)


def _attention(qt, k, vt, table):
    bsz, _, s, _ = k.shape
    t = ATT_TILE
    prev_idx = lambda i: jnp.maximum(i * (t // PAD) - 1, 0)
    return pl.pallas_call(
        _attn_kernel,
        grid=(HEAD_PAIRS, bsz, s // t),
        in_specs=[
            pl.BlockSpec((1, LANES, t), lambda p, b, i: (b, p, i)),
            pl.BlockSpec((1, 1, PAD, LANES), lambda p, b, i: (b, p, prev_idx(i), 0)),
            pl.BlockSpec((1, 1, t, LANES), lambda p, b, i: (b, p, i, 0)),
            pl.BlockSpec((1, LANES, PAD), lambda p, b, i: (b, p, prev_idx(i))),
            pl.BlockSpec((1, LANES, t), lambda p, b, i: (b, p, i)),
            pl.BlockSpec((N_VARIANTS, 2, ATT_WIN, ATT_SUB), lambda p, b, i: (0, p, 0, 0)),
        ],
        out_specs=pl.BlockSpec((1, 1, t, LANES), lambda p, b, i: (b, p, i, 0)),
        out_shape=jax.ShapeDtypeStruct(k.shape, BF16),
        scratch_shapes=[pltpu.VMEM((PAD + t, LANES), BF16), pltpu.VMEM((2 * VT_ROWS, PAD + t), BF16),
                        pltpu.VMEM((ATT_SLOTS, ATT_WIN, ATT_SUB), F32),
                        pltpu.VMEM((ATT_SLOTS, ATT_WIN, ATT_SUB), BF16)],
        compiler_params=pltpu.CompilerParams(
            dimension_semantics=("arbitrary", "arbitrary", "arbitrary"), vmem_limit_bytes=VMEM_LIMIT),
        name="band_attention",
    )(qt, k, k, vt, vt, table)


def kernel(x, a_norm, a_w_in, a_conv_w, a_conv_b, a_w_gate, a_b_gate, a_lambda, a_w_out, kv_norm, w_kv, k_norm, b_norm, b_w_q, b_q_norm, b_rel_bias, b_w_o, mlp_norm, w_up, w_down):
    bsz, s, d = x.shape
    assert d == D_MODEL and s % ATT_TILE == 0 and s % (L0_SUBS * REC_TILE) == 0 and s % KVQ_TILE == 0
    assert s % MLP_TILE == 0
    assert a_norm.shape[0] == 1 and b_norm.shape[0] == 1 and mlp_norm.shape[0] == 2
    n = bsz * s
    row = lambda p: p.reshape(1, -1).astype(F32)

    h = _layer0(x.reshape(n, d), s // (L0_SUBS * REC_TILE), row(a_norm[0]), a_w_in[0].astype(BF16), a_conv_w[0],
                row(a_conv_b[0]), a_w_gate[0].astype(BF16), a_b_gate[0], row(a_lambda[0]),
                a_w_out[0].astype(BF16), row(mlp_norm[0]), w_up[0].astype(BF16), w_down[0].astype(BF16))

    col = lambda g: jnp.broadcast_to(jnp.tile(g, N_HEADS).astype(F32)[:, None], (D_MODEL, LANES))
    q_gain = col(b_q_norm[0]) * (HEAD_DIM ** -0.5 * LOG2E)
    k_gain = col(k_norm)
    shp = (bsz, s, d)
    qt, k, vt = _kvq(h.reshape(shp), row(kv_norm), row(b_norm[0]), w_kv[:, :D_MODEL].T.astype(BF16),
                     w_kv[:, D_MODEL:].T.astype(BF16), b_w_q[0].T.astype(BF16), k_gain, q_gain)
    table = _bias_table(jnp.pad(b_rel_bias[0], ((0, 0), (0, 2 * LANES - NREL))))
    attn = _attention(qt, k, vt, table)
    out = _proj_mlp(attn, b_w_o[0].astype(BF16), h, row(mlp_norm[1]),
                    w_up[1].astype(BF16), w_down[1].astype(BF16))
    return out.reshape(shp)
```

```python
import functools

import jax
import jax.numpy as jnp
from jax import lax
from jax.experimental import pallas as pl
from jax.experimental.pallas import tpu as pltpu

D_MODEL = 1024
D_RNN = D_MODEL
LRU_BLOCKS = 8
LRU_BW = D_RNN // LRU_BLOCKS
CONV_W = 4
LRU_C = 8.0
N_HEADS = 16
HEAD_DIM = 64
CHUNK = 64
LEFT_CHUNKS = 8
PAD = LEFT_CHUNKS * CHUNK
MAX_REL = 2 * CHUNK
MIN_REL = -(CHUNK - 1)
NREL = MAX_REL - MIN_REL + 1
D_FF = 4 * D_MODEL
EPS = 1e-6

LANES = 128
SUBLANES = 8
TINY = 1.1754944e-38
NEG = -1e30

REC_TILE = 256
L0_SUBS = 2
L0_FF_CHUNK = 1024
L0_ORDER = ("rec", "conv", "g0", "m", "g1", "gate", "m", "s0", "s1", "m", "s2", "s3", "gelu", "m", "out")
MLP_TILE = 1024
KVQ_TILE = 1024
KVQ_SUB = 512
ATT_TILE = 8192
ATT_SUB = 256
ATT_WIN = PAD + ATT_SUB
ATT_LOOKAHEAD = 4
ATT_SLOTS = ATT_LOOKAHEAD + 2
ATT_KEY_ROWS = 384
VT_ROWS = HEAD_DIM + 16
LOG2E = 1.4426950408889634
HEAD_PAIRS = N_HEADS // 2
VMEM_LIMIT = 56 * 1024 * 1024

F32 = jnp.float32
BF16 = jnp.bfloat16


def _dot(a, b):
    return jnp.dot(a, b, preferred_element_type=F32)


def _rms_scale(x):
    return lax.rsqrt(jnp.mean(x * x, axis=-1, keepdims=True) + EPS)


def _const_spec(shape):
    zeros = (0,) * len(shape)
    return pl.BlockSpec(shape, lambda *_: zeros, pipeline_mode=pl.Buffered(1))


def _layer0_kernel(x_ref, norm_ref, w_in_ref, cw_ref, cb_ref, wg_ref, bg_ref, lam_ref, w_out_ref,
                   mnorm_ref, w_up_ref, w_down_ref, o_ref,
                   rec_ext, a_s, b_s, h_carry, mix_s, *, tiles_per_seq):
    t = REC_TILE
    j = pl.program_id(0)

    @pl.when(j % tiles_per_seq == 0)
    def _():
        rec_ext[:, 0:SUBLANES, :] = jnp.zeros((D_RNN // LANES, SUBLANES, LANES), F32)
        h_carry[...] = jnp.zeros_like(h_carry)

    @pl.when(j == 0)
    def _():
        mix_s[1] = jnp.zeros((L0_SUBS, t, D_MODEL), F32)

    slot = j % 2
    row = lax.broadcasted_iota(jnp.int32, (SUBLANES, D_RNN), 0)
    n_slabs = t // SUBLANES
    h_state = h_carry[...]
    for q in range(L0_SUBS):
        h_state = _layer0_sub_tile(
            q, slot, h_state, row, n_slabs, x_ref, norm_ref, w_in_ref, cw_ref, cb_ref, wg_ref, bg_ref,
            lam_ref, w_out_ref, mnorm_ref, w_up_ref, w_down_ref, o_ref, rec_ext, a_s, b_s, mix_s)
    h_carry[...] = h_state


def _layer0_sub_tile(q, slot, h_in, row, n_slabs, x_ref, norm_ref, w_in_ref, cw_ref, cb_ref, wg_ref,
                     bg_ref, lam_ref, w_out_ref, mnorm_ref, w_up_ref, w_down_ref, o_ref,
                     rec_ext, a_s, b_s, mix_s):
    t = REC_TILE
    rows_q = slice(q * t, (q + 1) * t)

    hp = mix_s[1 - slot, q]
    hn = (hp * _rms_scale(hp) * mnorm_ref[...]).astype(BF16)

    mlp_state = {"acc": hp, "k": 0}

    def mlp_chunk():
        k = mlp_state["k"]
        sl = slice(k * L0_FF_CHUNK, (k + 1) * L0_FF_CHUNK)
        up = jnp.maximum(_dot(hn, w_up_ref[:, sl]), 0.0)
        mlp_state["acc"] = mlp_state["acc"] + _dot((up * up).astype(BF16), w_down_ref[sl, :])
        mlp_state["k"] = k + 1

    x = x_ref[rows_q, :]
    xn = (x * _rms_scale(x) * norm_ref[...]).astype(BF16)
    v = {"h": h_in}

    def rec_proj():
        v["rec"] = _dot(xn, w_in_ref[:, D_RNN:])

    def gate_proj():
        v["gate"] = _dot(xn, w_in_ref[:, :D_RNN])

    def conv():
        cs = []
        for n in range(D_RNN // LANES):
            sl = slice(n * LANES, (n + 1) * LANES)
            rec = v["rec"][:, sl]
            rec_ext[n, SUBLANES:SUBLANES + t, :] = rec
            c = cb_ref[:, sl] + rec * cw_ref[CONV_W - 1:CONV_W, sl]
            for k in range(1, CONV_W):
                c = c + rec_ext[n, pl.ds(SUBLANES - k, t), :] * cw_ref[CONV_W - 1 - k:CONV_W - k, sl]
            cs.append(c)
            rec_ext[n, 0:SUBLANES, :] = rec_ext[n, t:t + SUBLANES, :]
        v["c"] = cs

    def gates(half):
        neg_lam = -lam_ref[...]
        softplus = jnp.maximum(neg_lam, 0.0) + jnp.log1p(jnp.exp(-jnp.abs(neg_lam)))
        for n in range(half * LRU_BLOCKS // 2, (half + 1) * LRU_BLOCKS // 2):
            sl = slice(n * LRU_BW, (n + 1) * LRU_BW)
            cb = v["c"][n]
            g = _dot(cb.astype(BF16), wg_ref[n]) + bg_ref[n:n + 1, :]
            r = jax.nn.sigmoid(g[:, :LRU_BW])
            i = jax.nn.sigmoid(g[:, LRU_BW:])
            log_a = (-LRU_C) * r * softplus[:, sl]
            a = jnp.exp(log_a)
            z = 1.0 - a * a
            mult = z * lax.rsqrt(jnp.maximum(z, TINY))
            a_s[q, :, sl] = a
            b_s[q, :, sl] = mult * (i * cb)

    def scan(quarter):
        for s in range(quarter * n_slabs // 4, (quarter + 1) * n_slabs // 4):
            rows = slice(s * SUBLANES, (s + 1) * SUBLANES)
            a = a_s[q, rows, :]
            b = b_s[q, rows, :]
            for d in (1, 2, 4):
                keep = row >= d
                a_sh = jnp.where(keep, pltpu.roll(a, d, 0), 1.0)
                b_sh = jnp.where(keep, pltpu.roll(b, d, 0), 0.0)
                b = a * b_sh + b
                a = a * a_sh
            h = a * v["h"] + b
            b_s[q, rows, :] = h
            v["h"] = h[SUBLANES - 1:SUBLANES, :]

    def gelu_gate():
        v["y"] = (jax.nn.gelu(v["gate"]) * b_s[q]).astype(BF16)

    def out_proj():
        mix_s[slot, q] = x + _dot(v["y"], w_out_ref[...])

    phases = {
        "rec": rec_proj, "gate": gate_proj, "conv": conv, "g0": lambda: gates(0), "g1": lambda: gates(1),
        "s0": lambda: scan(0), "s1": lambda: scan(1), "s2": lambda: scan(2), "s3": lambda: scan(3),
        "gelu": gelu_gate, "out": out_proj, "m": mlp_chunk,
    }
    for name in L0_ORDER:
        phases[name]()
    assert mlp_state["k"] == D_FF // L0_FF_CHUNK
    o_ref[rows_q, :] = mlp_state["acc"]
    return v["h"]


def _layer0(x2d, tiles_per_seq, norm, w_in, conv_w, conv_b, w_gate, b_gate, lam, w_out, mlp_norm, w_up, w_down):
    t = REC_TILE
    blk = L0_SUBS * t
    n_tiles = x2d.shape[0] // blk
    return pl.pallas_call(
        functools.partial(_layer0_kernel, tiles_per_seq=tiles_per_seq),
        grid=(n_tiles + 1,),
        in_specs=[
            pl.BlockSpec((blk, D_MODEL), lambda j: (jnp.minimum(j, n_tiles - 1), 0)),
            _const_spec((1, D_MODEL)),
            _const_spec((D_MODEL, 2 * D_RNN)),
            _const_spec((CONV_W, D_RNN)),
            _const_spec((1, D_RNN)),
            _const_spec((LRU_BLOCKS, LRU_BW, 2 * LRU_BW)),
            _const_spec((LRU_BLOCKS, 2 * LRU_BW)),
            _const_spec((1, D_RNN)),
            _const_spec((D_RNN, D_MODEL)),
            _const_spec((1, D_MODEL)),
            _const_spec((D_MODEL, D_FF)),
            _const_spec((D_FF, D_MODEL)),
        ],
        out_specs=pl.BlockSpec((blk, D_MODEL), lambda j: (jnp.maximum(j - 1, 0), 0)),
        out_shape=jax.ShapeDtypeStruct(x2d.shape, F32),
        scratch_shapes=[
            pltpu.VMEM((D_RNN // LANES, t + SUBLANES, LANES), F32),
            pltpu.VMEM((L0_SUBS, t, D_RNN), F32),
            pltpu.VMEM((L0_SUBS, t, D_RNN), F32),
            pltpu.VMEM((1, D_RNN), F32),
            pltpu.VMEM((2, L0_SUBS, t, D_MODEL), F32),
        ],
        compiler_params=pltpu.CompilerParams(
            dimension_semantics=("arbitrary",), vmem_limit_bytes=VMEM_LIMIT),
        name="layer0_mixer_mlp",
    )(x2d, norm, w_in, conv_w, conv_b, w_gate, b_gate, lam, w_out, mlp_norm, w_up, w_down)


FF_CHUNK = 1024


def _mlp_body(h, norm_ref, w_up_ref, w_down_ref):
    hn = (h * _rms_scale(h) * norm_ref[...]).astype(BF16)
    acc = h
    for j in range(D_FF // FF_CHUNK):
        sl = slice(j * FF_CHUNK, (j + 1) * FF_CHUNK)
        up = jnp.maximum(_dot(hn, w_up_ref[:, sl]), 0.0)
        acc = acc + _dot((up * up).astype(BF16), w_down_ref[sl, :])
    return acc


def _proj_mlp_kernel(a_ref, w_o_ref, x_ref, norm_ref, w_up_ref, w_down_ref, o_ref):
    attn = jnp.concatenate([a_ref[0, p] for p in range(HEAD_PAIRS)], axis=1)
    h = x_ref[...] + _dot(attn, w_o_ref[...])
    o_ref[...] = _mlp_body(h, norm_ref, w_up_ref, w_down_ref)


def _proj_mlp(attn, w_o, x2d, norm, w_up, w_down):
    n = x2d.shape[0]
    t = MLP_TILE
    tiles_per_seq = attn.shape[2] // t
    tile = pl.BlockSpec((t, D_MODEL), lambda i: (i, 0))
    attn_tile = pl.BlockSpec((1, HEAD_PAIRS, t, LANES), lambda i: (i // tiles_per_seq, 0, i % tiles_per_seq, 0))
    return pl.pallas_call(
        _proj_mlp_kernel,
        grid=(n // t,),
        in_specs=[attn_tile, _const_spec((D_MODEL, D_MODEL)), tile, _const_spec((1, D_MODEL)),
                  _const_spec((D_MODEL, D_FF)), _const_spec((D_FF, D_MODEL))],
        out_specs=tile,
        out_shape=jax.ShapeDtypeStruct(x2d.shape, F32),
        compiler_params=pltpu.CompilerParams(
            dimension_semantics=("arbitrary",), vmem_limit_bytes=VMEM_LIMIT),
        name="proj_mlp",
    )(attn, w_o, x2d, norm, w_up, w_down)


def _dot_nt(a, b):
    return lax.dot_general(a, b, (((1,), (1,)), ((), ())), preferred_element_type=F32)


def _head_rms_normalise_t(zt, gain_ref):
    t = zt.shape[1]
    z3 = zt.reshape(N_HEADS, HEAD_DIM, t)
    inv = lax.rsqrt(jnp.mean(z3 * z3, axis=1, keepdims=True) + EPS)
    gain = jnp.concatenate([gain_ref[...]] * (t // LANES), axis=1)
    return (z3 * inv).reshape(N_HEADS * HEAD_DIM, t) * gain


def _kvq_kernel(x_ref, kvn_ref, qn_ref, w_kt_ref, w_vt_ref, w_qt_ref, kg_ref, qg_ref,
                qt_ref, k_ref, vt_ref):
    def project(r0):
        x = x_ref[0, r0:r0 + KVQ_SUB, :]
        xs = x * _rms_scale(x)
        xkv = (xs * kvn_ref[...]).astype(BF16)
        xq = (xs * qn_ref[...]).astype(BF16)
        return _dot_nt(w_vt_ref[...], xkv), _dot_nt(w_kt_ref[...], xkv), _dot_nt(w_qt_ref[...], xq)

    def finish(r0, vt, kt, qt):
        rows = slice(r0, r0 + KVQ_SUB)
        vt_ref[0, :, rows] = vt.astype(BF16)
        kn = _head_rms_normalise_t(kt, kg_ref).T.astype(BF16)
        for p in range(HEAD_PAIRS):
            k_ref[0, p, rows, :] = kn[:, p * LANES:(p + 1) * LANES]
        qt_ref[0, :, rows] = _head_rms_normalise_t(qt, qg_ref).astype(BF16)

    starts = list(range(0, KVQ_TILE, KVQ_SUB))
    pending = project(starts[0])
    for n, r0 in enumerate(starts):
        nxt = project(starts[n + 1]) if n + 1 < len(starts) else None
        finish(r0, *pending)
        pending = nxt


def _kvq(x, kv_norm, q_norm, w_kt, w_vt, w_qt, k_gain, q_gain):
    bsz, s, _ = x.shape
    t = KVQ_TILE
    tok_major = pl.BlockSpec((1, t, D_MODEL), lambda b, i: (b, i, 0))
    feat_major = pl.BlockSpec((1, D_MODEL, t), lambda b, i: (b, 0, i))
    w_spec = _const_spec((D_MODEL, D_MODEL))
    g_spec = _const_spec((D_MODEL, LANES))
    ft = jax.ShapeDtypeStruct((bsz, D_MODEL, s), BF16)
    return pl.pallas_call(
        _kvq_kernel,
        grid=(bsz, s // t),
        in_specs=[tok_major, _const_spec((1, D_MODEL)), _const_spec((1, D_MODEL)),
                  w_spec, w_spec, w_spec, g_spec, g_spec],
        out_specs=[feat_major, pl.BlockSpec((1, HEAD_PAIRS, t, LANES), lambda b, i: (b, 0, i, 0)), feat_major],
        out_shape=[ft, jax.ShapeDtypeStruct((bsz, HEAD_PAIRS, s, LANES), BF16), ft],
        compiler_params=pltpu.CompilerParams(
            dimension_semantics=("arbitrary", "arbitrary"), vmem_limit_bytes=VMEM_LIMIT),
        name="kvq_proj",
    )(x, kv_norm, q_norm, w_kt, w_vt, w_qt, k_gain, q_gain)


BASE_W = ATT_WIN + ATT_SUB
N_VARIANTS = 1 + PAD // ATT_SUB
ROLL_ROWS = 128
TABLE_HEADS = 4


def _bias_kernel(rb_ref, o_ref, base_s):
    rb = rb_ref[...] * LOG2E
    r_idx = lax.broadcasted_iota(jnp.int32, (2 * LANES, BASE_W), 0)
    m_idx = lax.broadcasted_iota(jnp.int32, (2 * LANES, BASE_W), 1)
    want = jnp.clip(m_idx - ATT_SUB, MIN_REL, MAX_REL) - MIN_REL
    onehot = (r_idx == want).astype(BF16)
    p0 = rb.astype(BF16)
    r1 = rb - p0.astype(F32)
    p1 = r1.astype(BF16)
    p2 = (r1 - p1.astype(F32)).astype(BF16)
    base_s[...] = _dot(p0, onehot) + _dot(p1, onehot) + _dot(p2, onehot)
    for hl in range(TABLE_HEADS):
        base = base_s[pl.ds(pl.program_id(0) * TABLE_HEADS + hl, 1), :]
        for r0 in range(0, ATT_WIN, ROLL_ROWS):
            kj = r0 + lax.broadcasted_iota(jnp.int32, (ROLL_ROWS, ATT_SUB), 0)
            qi = lax.broadcasted_iota(jnp.int32, (ROLL_ROWS, ATT_SUB), 1)
            band_lo = (qi // CHUNK) * CHUNK
            in_band = (kj >= band_lo) & (kj < band_lo + PAD + CHUNK)
            rows = jnp.broadcast_to(base, (ROLL_ROWS, BASE_W))
            toeplitz = pltpu.roll(rows, r0, 1, stride=1, stride_axis=0)
            tile = jnp.where(in_band, toeplitz[:, ATT_WIN:], NEG)
            o_ref[0, hl, r0:r0 + ROLL_ROWS, :] = tile
            for v in range(1, N_VARIANTS):
                o_ref[v, hl, r0:r0 + ROLL_ROWS, :] = jnp.where(kj >= PAD - (v - 1) * ATT_SUB, tile, NEG)


def _bias_table(rel_bias_padded):
    return pl.pallas_call(
        _bias_kernel,
        grid=(N_HEADS // TABLE_HEADS,),
        in_specs=[_const_spec((N_HEADS, 2 * LANES))],
        out_specs=pl.BlockSpec((N_VARIANTS, TABLE_HEADS, ATT_WIN, ATT_SUB), lambda h: (0, h, 0, 0)),
        out_shape=jax.ShapeDtypeStruct((N_VARIANTS, N_HEADS, ATT_WIN, ATT_SUB), F32),
        scratch_shapes=[pltpu.VMEM((N_HEADS, BASE_W), F32)],
        compiler_params=pltpu.CompilerParams(
            dimension_semantics=("arbitrary",), vmem_limit_bytes=VMEM_LIMIT),
        name="rel_bias_table",
    )(rel_bias_padded)


def _attn_kernel(qt_ref, kp_ref, kc_ref, vtp_ref, vtc_ref, tbl_ref, o_ref, kcat, vtcat, st_s, p_s):
    i = pl.program_id(2)
    kcat[0:PAD, :] = kp_ref[0, 0]
    kcat[PAD:, :] = kc_ref[0, 0]
    for hh in range(2):
        src = slice(hh * HEAD_DIM, (hh + 1) * HEAD_DIM)
        r0 = hh * VT_ROWS
        vtcat[r0:r0 + HEAD_DIM, 0:PAD] = vtp_ref[0, src, :]
        vtcat[r0:r0 + HEAD_DIM, PAD:] = vtc_ref[0, src, :]
        vtcat[r0 + HEAD_DIM:r0 + VT_ROWS, :] = jnp.ones((VT_ROWS - HEAD_DIM, PAD + ATT_TILE), BF16)

    feat = lax.broadcasted_iota(jnp.int32, (LANES, ATT_SUB), 0)
    first_head = feat < HEAD_DIM
    n_blk = ATT_TILE // ATT_SUB
    units = [(blk, hh) for blk in range(n_blk) for hh in range(2)]

    def scores(u):
        blk, hh = units[u]
        c0 = blk * ATT_SUB
        qt = qt_ref[0, :, c0:c0 + ATT_SUB]
        kwin = kcat[c0:c0 + ATT_WIN, :]
        variant = jnp.where(i == 0, blk + 1, 0) if blk + 1 < N_VARIANTS else 0
        mask = first_head if hh == 0 else jnp.logical_not(first_head)
        qm = jnp.where(mask, qt, jnp.zeros_like(qt))
        m = None
        for r0 in range(0, ATT_WIN, ATT_KEY_ROWS):
            rows = slice(r0, r0 + ATT_KEY_ROWS)
            st = _dot(kwin[rows], qm) + tbl_ref[variant, hh, rows, :]
            st_s[u % ATT_SLOTS, rows, :] = st
            piece_max = jnp.max(st, axis=0, keepdims=True)
            m = piece_max if m is None else jnp.maximum(m, piece_max)
        return m

    def softmax_pv(u, m):
        blk, hh = units[u]
        c0 = blk * ATT_SUB
        p_s[u % ATT_SLOTS] = jnp.exp2((st_s[u % ATT_SLOTS] - m).astype(BF16))
        vt = vtcat[hh * VT_ROWS:(hh + 1) * VT_ROWS, c0:c0 + ATT_WIN]
        ot = _dot(vt, p_s[u % ATT_SLOTS])
        return ot[:HEAD_DIM] * (1.0 / ot[HEAD_DIM:HEAD_DIM + 1])

    outs = []
    ms = [scores(u) for u in range(ATT_LOOKAHEAD)]
    for u in range(len(units)):
        if u + ATT_LOOKAHEAD < len(units):
            ms.append(scores(u + ATT_LOOKAHEAD))
        outs.append(softmax_pv(u, ms[u]))
        if u % 2 == 1:
            c0 = units[u][0] * ATT_SUB
            pair = jnp.concatenate(outs[-2:], axis=0)
            o_ref[0, 0, c0:c0 + ATT_SUB, :] = pair.T.astype(BF16)


def _attention(qt, k, vt, table):
    bsz, _, s, _ = k.shape
    t = ATT_TILE
    prev_idx = lambda i: jnp.maximum(i * (t // PAD) - 1, 0)
    return pl.pallas_call(
        _attn_kernel,
        grid=(HEAD_PAIRS, bsz, s // t),
        in_specs=[
            pl.BlockSpec((1, LANES, t), lambda p, b, i: (b, p, i)),
            pl.BlockSpec((1, 1, PAD, LANES), lambda p, b, i: (b, p, prev_idx(i), 0)),
            pl.BlockSpec((1, 1, t, LANES), lambda p, b, i: (b, p, i, 0)),
            pl.BlockSpec((1, LANES, PAD), lambda p, b, i: (b, p, prev_idx(i))),
            pl.BlockSpec((1, LANES, t), lambda p, b, i: (b, p, i)),
            pl.BlockSpec((N_VARIANTS, 2, ATT_WIN, ATT_SUB), lambda p, b, i: (0, p, 0, 0)),
        ],
        out_specs=pl.BlockSpec((1, 1, t, LANES), lambda p, b, i: (b, p, i, 0)),
        out_shape=jax.ShapeDtypeStruct(k.shape, BF16),
        scratch_shapes=[pltpu.VMEM((PAD + t, LANES), BF16), pltpu.VMEM((2 * VT_ROWS, PAD + t), BF16),
                        pltpu.VMEM((ATT_SLOTS, ATT_WIN, ATT_SUB), F32),
                        pltpu.VMEM((ATT_SLOTS, ATT_WIN, ATT_SUB), BF16)],
        compiler_params=pltpu.CompilerParams(
            dimension_semantics=("arbitrary", "arbitrary", "arbitrary"), vmem_limit_bytes=VMEM_LIMIT),
        name="band_attention",
    )(qt, k, k, vt, vt, table)


def kernel(x, a_norm, a_w_in, a_conv_w, a_conv_b, a_w_gate, a_b_gate, a_lambda, a_w_out, kv_norm, w_kv, k_norm, b_norm, b_w_q, b_q_norm, b_rel_bias, b_w_o, mlp_norm, w_up, w_down):
    bsz, s, d = x.shape
    assert d == D_MODEL and s % ATT_TILE == 0 and s % (L0_SUBS * REC_TILE) == 0 and s % KVQ_TILE == 0
    assert s % MLP_TILE == 0
    assert a_norm.shape[0] == 1 and b_norm.shape[0] == 1 and mlp_norm.shape[0] == 2
    n = bsz * s
    row = lambda p: p.reshape(1, -1).astype(F32)

    h = _layer0(x.reshape(n, d), s // (L0_SUBS * REC_TILE), row(a_norm[0]), a_w_in[0].astype(BF16), a_conv_w[0],
                row(a_conv_b[0]), a_w_gate[0].astype(BF16), a_b_gate[0], row(a_lambda[0]),
                a_w_out[0].astype(BF16), row(mlp_norm[0]), w_up[0].astype(BF16), w_down[0].astype(BF16))

    col = lambda g: jnp.broadcast_to(jnp.tile(g, N_HEADS).astype(F32)[:, None], (D_MODEL, LANES))
    q_gain = col(b_q_norm[0]) * (HEAD_DIM ** -0.5 * LOG2E)
    k_gain = col(k_norm)
    shp = (bsz, s, d)
    qt, k, vt = _kvq(h.reshape(shp), row(kv_norm), row(b_norm[0]), w_kv[:, :D_MODEL].T.astype(BF16),
                     w_kv[:, D_MODEL:].T.astype(BF16), b_w_q[0].T.astype(BF16), k_gain, q_gain)
    table = _bias_table(jnp.pad(b_rel_bias[0], ((0, 0), (0, 2 * LANES - NREL))))
    attn = _attention(qt, k, vt, table)
    out = _proj_mlp(attn, b_w_o[0].astype(BF16), h, row(mlp_norm[1]),
                    w_up[1].astype(BF16), w_down[1].astype(BF16))
    return out.reshape(shp)
```

```python
import functools

import jax
import jax.numpy as jnp
import numpy as np
from jax import lax
from jax.experimental import pallas as pl
from jax.experimental.pallas import tpu as pltpu

D_MODEL = 1024
D_RNN = D_MODEL
LRU_BLOCKS = 8
LRU_BW = D_RNN // LRU_BLOCKS
CONV_W = 4
LRU_C = 8.0
N_HEADS = 16
HEAD_DIM = 64
CHUNK = 64
LEFT_CHUNKS = 8
PAD = LEFT_CHUNKS * CHUNK
MAX_REL = 2 * CHUNK
MIN_REL = -(CHUNK - 1)
NREL = MAX_REL - MIN_REL + 1
D_FF = 4 * D_MODEL
EPS = 1e-6

LANES = 128
SUBLANES = 8
TINY = 1.1754944e-38
NEG = -1e30

REC_TILE = 256
L0_SUBS = 2
L0_FF_CHUNK = 1024
L0_ORDER = ("rec", "conv", "g0", "m", "g1", "gate", "m", "s0", "s1", "m", "s2", "s3", "gelu", "m", "out")
MLP_TILE = 1024
KVQ_TILE = 1024
KVQ_SUB = 512
ATT_TILE = 8192
ATT_SUB = 256
ATT_WIN = PAD + ATT_SUB
ATT_LOOKAHEAD = 4
ATT_SLOTS = ATT_LOOKAHEAD + 2
ATT_KEY_ROWS = 384
VT_ROWS = HEAD_DIM + 16
LOG2E = 1.4426950408889634
HEAD_PAIRS = N_HEADS // 2
VMEM_LIMIT = 56 * 1024 * 1024

F32 = jnp.float32
BF16 = jnp.bfloat16


def _dot(a, b):
    return jnp.dot(a, b, preferred_element_type=F32)


def _rms_scale(x):
    return lax.rsqrt(jnp.mean(x * x, axis=-1, keepdims=True) + EPS)


def _const_spec(shape):
    zeros = (0,) * len(shape)
    return pl.BlockSpec(shape, lambda *_: zeros, pipeline_mode=pl.Buffered(1))


def _layer0_kernel(x_ref, norm_ref, w_in_ref, cw_ref, cb_ref, wg_ref, bg_ref, lam_ref, w_out_ref,
                   mnorm_ref, w_up_ref, w_down_ref, o_ref,
                   rec_ext, a_s, b_s, h_carry, mix_s, *, tiles_per_seq):
    t = REC_TILE
    j = pl.program_id(0)

    @pl.when(j % tiles_per_seq == 0)
    def _():
        rec_ext[:, 0:SUBLANES, :] = jnp.zeros((D_RNN // LANES, SUBLANES, LANES), F32)
        h_carry[...] = jnp.zeros_like(h_carry)

    @pl.when(j == 0)
    def _():
        mix_s[1] = jnp.zeros((L0_SUBS, t, D_MODEL), F32)

    slot = j % 2
    row = lax.broadcasted_iota(jnp.int32, (SUBLANES, D_RNN), 0)
    n_slabs = t // SUBLANES
    h_state = h_carry[...]
    for q in range(L0_SUBS):
        h_state = _layer0_sub_tile(
            q, slot, h_state, row, n_slabs, x_ref, norm_ref, w_in_ref, cw_ref, cb_ref, wg_ref, bg_ref,
            lam_ref, w_out_ref, mnorm_ref, w_up_ref, w_down_ref, o_ref, rec_ext, a_s, b_s, mix_s)
    h_carry[...] = h_state


def _layer0_sub_tile(q, slot, h_in, row, n_slabs, x_ref, norm_ref, w_in_ref, cw_ref, cb_ref, wg_ref,
                     bg_ref, lam_ref, w_out_ref, mnorm_ref, w_up_ref, w_down_ref, o_ref,
                     rec_ext, a_s, b_s, mix_s):
    t = REC_TILE
    rows_q = slice(q * t, (q + 1) * t)

    hp = mix_s[1 - slot, q]
    hn = (hp * _rms_scale(hp) * mnorm_ref[...]).astype(BF16)

    mlp_state = {"acc": hp, "k": 0}

    def mlp_chunk():
        k = mlp_state["k"]
        sl = slice(k * L0_FF_CHUNK, (k + 1) * L0_FF_CHUNK)
        up = jnp.maximum(_dot(hn, w_up_ref[:, sl]), 0.0)
        mlp_state["acc"] = mlp_state["acc"] + _dot((up * up).astype(BF16), w_down_ref[sl, :])
        mlp_state["k"] = k + 1

    x = x_ref[rows_q, :]
    xn = (x * _rms_scale(x) * norm_ref[...]).astype(BF16)
    v = {"h": h_in}

    def rec_proj():
        v["rec"] = _dot(xn, w_in_ref[:, D_RNN:])

    def gate_proj():
        v["gate"] = _dot(xn, w_in_ref[:, :D_RNN])

    def conv():
        cs = []
        for n in range(D_RNN // LANES):
            sl = slice(n * LANES, (n + 1) * LANES)
            rec = v["rec"][:, sl]
            rec_ext[n, SUBLANES:SUBLANES + t, :] = rec
            c = cb_ref[:, sl] + rec * cw_ref[CONV_W - 1:CONV_W, sl]
            for k in range(1, CONV_W):
                c = c + rec_ext[n, pl.ds(SUBLANES - k, t), :] * cw_ref[CONV_W - 1 - k:CONV_W - k, sl]
            cs.append(c)
            rec_ext[n, 0:SUBLANES, :] = rec_ext[n, t:t + SUBLANES, :]
        v["c"] = cs

    def gates(half):
        neg_lam = -lam_ref[...]
        softplus = jnp.maximum(neg_lam, 0.0) + jnp.log1p(jnp.exp(-jnp.abs(neg_lam)))
        for n in range(half * LRU_BLOCKS // 2, (half + 1) * LRU_BLOCKS // 2):
            sl = slice(n * LRU_BW, (n + 1) * LRU_BW)
            cb = v["c"][n]
            g = _dot(cb.astype(BF16), wg_ref[n]) + bg_ref[n:n + 1, :]
            r = jax.nn.sigmoid(g[:, :LRU_BW])
            i = jax.nn.sigmoid(g[:, LRU_BW:])
            log_a = (-LRU_C) * r * softplus[:, sl]
            a = jnp.exp(log_a)
            z = 1.0 - a * a
            mult = z * lax.rsqrt(jnp.maximum(z, TINY))
            a_s[q, :, sl] = a
            b_s[q, :, sl] = mult * (i * cb)

    def scan(quarter):
        for s in range(quarter * n_slabs // 4, (quarter + 1) * n_slabs // 4):
            rows = slice(s * SUBLANES, (s + 1) * SUBLANES)
            a = a_s[q, rows, :]
            b = b_s[q, rows, :]
            for d in (1, 2, 4):
                keep = row >= d
                a_sh = jnp.where(keep, pltpu.roll(a, d, 0), 1.0)
                b_sh = jnp.where(keep, pltpu.roll(b, d, 0), 0.0)
                b = a * b_sh + b
                a = a * a_sh
            h = a * v["h"] + b
            b_s[q, rows, :] = h
            v["h"] = h[SUBLANES - 1:SUBLANES, :]

    def gelu_gate():
        v["y"] = (jax.nn.gelu(v["gate"]) * b_s[q]).astype(BF16)

    def out_proj():
        mix_s[slot, q] = x + _dot(v["y"], w_out_ref[...])

    phases = {
        "rec": rec_proj, "gate": gate_proj, "conv": conv, "g0": lambda: gates(0), "g1": lambda: gates(1),
        "s0": lambda: scan(0), "s1": lambda: scan(1), "s2": lambda: scan(2), "s3": lambda: scan(3),
        "gelu": gelu_gate, "out": out_proj, "m": mlp_chunk,
    }
    for name in L0_ORDER:
        phases[name]()
    assert mlp_state["k"] == D_FF // L0_FF_CHUNK
    o_ref[rows_q, :] = mlp_state["acc"]
    return v["h"]


def _layer0(x2d, tiles_per_seq, norm, w_in, conv_w, conv_b, w_gate, b_gate, lam, w_out, mlp_norm, w_up, w_down):
    t = REC_TILE
    blk = L0_SUBS * t
    n_tiles = x2d.shape[0] // blk
    return pl.pallas_call(
        functools.partial(_layer0_kernel, tiles_per_seq=tiles_per_seq),
        grid=(n_tiles + 1,),
        in_specs=[
            pl.BlockSpec((blk, D_MODEL), lambda j: (jnp.minimum(j, n_tiles - 1), 0)),
            _const_spec((1, D_MODEL)),
            _const_spec((D_MODEL, 2 * D_RNN)),
            _const_spec((CONV_W, D_RNN)),
            _const_spec((1, D_RNN)),
            _const_spec((LRU_BLOCKS, LRU_BW, 2 * LRU_BW)),
            _const_spec((LRU_BLOCKS, 2 * LRU_BW)),
            _const_spec((1, D_RNN)),
            _const_spec((D_RNN, D_MODEL)),
            _const_spec((1, D_MODEL)),
            _const_spec((D_MODEL, D_FF)),
            _const_spec((D_FF, D_MODEL)),
        ],
        out_specs=pl.BlockSpec((blk, D_MODEL), lambda j: (jnp.maximum(j - 1, 0), 0)),
        out_shape=jax.ShapeDtypeStruct(x2d.shape, F32),
        scratch_shapes=[
            pltpu.VMEM((D_RNN // LANES, t + SUBLANES, LANES), F32),
            pltpu.VMEM((L0_SUBS, t, D_RNN), F32),
            pltpu.VMEM((L0_SUBS, t, D_RNN), F32),
            pltpu.VMEM((1, D_RNN), F32),
            pltpu.VMEM((2, L0_SUBS, t, D_MODEL), F32),
        ],
        compiler_params=pltpu.CompilerParams(
            dimension_semantics=("arbitrary",), vmem_limit_bytes=VMEM_LIMIT),
        name="layer0_mixer_mlp",
    )(x2d, norm, w_in, conv_w, conv_b, w_gate, b_gate, lam, w_out, mlp_norm, w_up, w_down)


FF_CHUNK = 1024


def _mlp_body(h, norm_ref, w_up_ref, w_down_ref):
    hn = (h * _rms_scale(h) * norm_ref[...]).astype(BF16)
    acc = h
    for j in range(D_FF // FF_CHUNK):
        sl = slice(j * FF_CHUNK, (j + 1) * FF_CHUNK)
        up = jnp.maximum(_dot(hn, w_up_ref[:, sl]), 0.0)
        acc = acc + _dot((up * up).astype(BF16), w_down_ref[sl, :])
    return acc


def _proj_mlp_kernel(a_ref, w_o_ref, x_ref, norm_ref, w_up_ref, w_down_ref, o_ref):
    attn = jnp.concatenate([a_ref[0, p] for p in range(HEAD_PAIRS)], axis=1)
    h = x_ref[...] + _dot(attn, w_o_ref[...])
    o_ref[...] = _mlp_body(h, norm_ref, w_up_ref, w_down_ref)


def _proj_mlp(attn, w_o, x2d, norm, w_up, w_down):
    n = x2d.shape[0]
    t = MLP_TILE
    tiles_per_seq = attn.shape[2] // t
    tile = pl.BlockSpec((t, D_MODEL), lambda i: (i, 0))
    attn_tile = pl.BlockSpec((1, HEAD_PAIRS, t, LANES), lambda i: (i // tiles_per_seq, 0, i % tiles_per_seq, 0))
    return pl.pallas_call(
        _proj_mlp_kernel,
        grid=(n // t,),
        in_specs=[attn_tile, _const_spec((D_MODEL, D_MODEL)), tile, _const_spec((1, D_MODEL)),
                  _const_spec((D_MODEL, D_FF)), _const_spec((D_FF, D_MODEL))],
        out_specs=tile,
        out_shape=jax.ShapeDtypeStruct(x2d.shape, F32),
        compiler_params=pltpu.CompilerParams(
            dimension_semantics=("arbitrary",), vmem_limit_bytes=VMEM_LIMIT),
        name="proj_mlp",
    )(attn, w_o, x2d, norm, w_up, w_down)


def _dot_nt(a, b):
    return lax.dot_general(a, b, (((1,), (1,)), ((), ())), preferred_element_type=F32)


def _head_rms_normalise_t(zt, gain_ref):
    t = zt.shape[1]
    z3 = zt.reshape(N_HEADS, HEAD_DIM, t)
    inv = lax.rsqrt(jnp.mean(z3 * z3, axis=1, keepdims=True) + EPS)
    gain = jnp.concatenate([gain_ref[...]] * (t // LANES), axis=1)
    return (z3 * inv).reshape(N_HEADS * HEAD_DIM, t) * gain


def _kvq_kernel(x_ref, kvn_ref, qn_ref, w_kt_ref, w_vt_ref, w_qt_ref, kg_ref, qg_ref,
                qt_ref, k_ref, vt_ref):
    def project(r0):
        x = x_ref[0, r0:r0 + KVQ_SUB, :]
        xs = x * _rms_scale(x)
        xkv = (xs * kvn_ref[...]).astype(BF16)
        xq = (xs * qn_ref[...]).astype(BF16)
        return _dot_nt(w_vt_ref[...], xkv), _dot_nt(w_kt_ref[...], xkv), _dot_nt(w_qt_ref[...], xq)

    def finish(r0, vt, kt, qt):
        rows = slice(r0, r0 + KVQ_SUB)
        vt_ref[0, :, rows] = vt.astype(BF16)
        kn = _head_rms_normalise_t(kt, kg_ref).T.astype(BF16)
        for p in range(HEAD_PAIRS):
            k_ref[0, p, rows, :] = kn[:, p * LANES:(p + 1) * LANES]
        qt_ref[0, :, rows] = _head_rms_normalise_t(qt, qg_ref).astype(BF16)

    starts = list(range(0, KVQ_TILE, KVQ_SUB))
    pending = project(starts[0])
    for n, r0 in enumerate(starts):
        nxt = project(starts[n + 1]) if n + 1 < len(starts) else None
        finish(r0, *pending)
        pending = nxt


def _kvq(x, kv_norm, q_norm, w_kt, w_vt, w_qt, k_gain, q_gain):
    bsz, s, _ = x.shape
    t = KVQ_TILE
    tok_major = pl.BlockSpec((1, t, D_MODEL), lambda b, i: (b, i, 0))
    feat_major = pl.BlockSpec((1, D_MODEL, t), lambda b, i: (b, 0, i))
    w_spec = _const_spec((D_MODEL, D_MODEL))
    g_spec = _const_spec((D_MODEL, LANES))
    ft = jax.ShapeDtypeStruct((bsz, D_MODEL, s), BF16)
    return pl.pallas_call(
        _kvq_kernel,
        grid=(bsz, s // t),
        in_specs=[tok_major, _const_spec((1, D_MODEL)), _const_spec((1, D_MODEL)),
                  w_spec, w_spec, w_spec, g_spec, g_spec],
        out_specs=[feat_major, pl.BlockSpec((1, HEAD_PAIRS, t, LANES), lambda b, i: (b, 0, i, 0)), feat_major],
        out_shape=[ft, jax.ShapeDtypeStruct((bsz, HEAD_PAIRS, s, LANES), BF16), ft],
        compiler_params=pltpu.CompilerParams(
            dimension_semantics=("arbitrary", "arbitrary"), vmem_limit_bytes=VMEM_LIMIT),
        name="kvq_proj",
    )(x, kv_norm, q_norm, w_kt, w_vt, w_qt, k_gain, q_gain)


BASE_W = ATT_WIN + ATT_SUB
N_VARIANTS = 1 + PAD // ATT_SUB
ROLL_ROWS = 128
TABLE_HEADS = 4


def _score_block_kinds():
    kj = np.arange(ATT_WIN)[:, None]
    qi = np.arange(ATT_SUB)[None, :]
    band_lo = (qi // CHUNK) * CHUNK
    in_band = (kj >= band_lo) & (kj < band_lo + PAD + CHUNK)
    far = (qi + PAD - kj) >= MAX_REL
    kinds = {}
    for rb in range(ATT_WIN // CHUNK):
        for lt in range(ATT_SUB // LANES):
            blk = (slice(rb * CHUNK, (rb + 1) * CHUNK), slice(lt * LANES, (lt + 1) * LANES))
            if not in_band[blk].any():
                kinds[rb, lt] = "masked"
            else:
                kinds[rb, lt] = "free" if (in_band[blk] & far[blk]).all() else "bias"
    return kinds


def _bias_kernel(rb_ref, o_ref, base_s):
    rb = (rb_ref[...] - rb_ref[:, NREL - 1:NREL]) * LOG2E
    r_idx = lax.broadcasted_iota(jnp.int32, (2 * LANES, BASE_W), 0)
    m_idx = lax.broadcasted_iota(jnp.int32, (2 * LANES, BASE_W), 1)
    want = jnp.clip(m_idx - ATT_SUB, MIN_REL, MAX_REL) - MIN_REL
    onehot = (r_idx == want).astype(BF16)
    p0 = rb.astype(BF16)
    r1 = rb - p0.astype(F32)
    p1 = r1.astype(BF16)
    p2 = (r1 - p1.astype(F32)).astype(BF16)
    base_s[...] = _dot(p0, onehot) + _dot(p1, onehot) + _dot(p2, onehot)
    for hl in range(TABLE_HEADS):
        base = base_s[pl.ds(pl.program_id(0) * TABLE_HEADS + hl, 1), :]
        for r0 in range(0, ATT_WIN, ROLL_ROWS):
            kj = r0 + lax.broadcasted_iota(jnp.int32, (ROLL_ROWS, ATT_SUB), 0)
            qi = lax.broadcasted_iota(jnp.int32, (ROLL_ROWS, ATT_SUB), 1)
            band_lo = (qi // CHUNK) * CHUNK
            in_band = (kj >= band_lo) & (kj < band_lo + PAD + CHUNK)
            rows = jnp.broadcast_to(base, (ROLL_ROWS, BASE_W))
            toeplitz = pltpu.roll(rows, r0, 1, stride=1, stride_axis=0)
            tile = jnp.where(in_band, toeplitz[:, ATT_WIN:], NEG)
            o_ref[0, hl, r0:r0 + ROLL_ROWS, :] = tile
            for v in range(1, N_VARIANTS):
                o_ref[v, hl, r0:r0 + ROLL_ROWS, :] = jnp.where(kj >= PAD - (v - 1) * ATT_SUB, tile, NEG)


def _bias_table(rel_bias_padded):
    return pl.pallas_call(
        _bias_kernel,
        grid=(N_HEADS // TABLE_HEADS,),
        in_specs=[_const_spec((N_HEADS, 2 * LANES))],
        out_specs=pl.BlockSpec((N_VARIANTS, TABLE_HEADS, ATT_WIN, ATT_SUB), lambda h: (0, h, 0, 0)),
        out_shape=jax.ShapeDtypeStruct((N_VARIANTS, N_HEADS, ATT_WIN, ATT_SUB), F32),
        scratch_shapes=[pltpu.VMEM((N_HEADS, BASE_W), F32)],
        compiler_params=pltpu.CompilerParams(
            dimension_semantics=("arbitrary",), vmem_limit_bytes=VMEM_LIMIT),
        name="rel_bias_table",
    )(rel_bias_padded)


def _attn_kernel(qt_ref, kp_ref, kc_ref, vtp_ref, vtc_ref, tbl_ref, o_ref, kcat, vtcat, st_s, p_s):
    i = pl.program_id(2)
    kcat[0:PAD, :] = kp_ref[0, 0]
    kcat[PAD:, :] = kc_ref[0, 0]
    for hh in range(2):
        src = slice(hh * HEAD_DIM, (hh + 1) * HEAD_DIM)
        r0 = hh * VT_ROWS
        vtcat[r0:r0 + HEAD_DIM, 0:PAD] = vtp_ref[0, src, :]
        vtcat[r0:r0 + HEAD_DIM, PAD:] = vtc_ref[0, src, :]
        vtcat[r0 + HEAD_DIM:r0 + VT_ROWS, :] = jnp.ones((VT_ROWS - HEAD_DIM, PAD + ATT_TILE), BF16)

    feat = lax.broadcasted_iota(jnp.int32, (LANES, ATT_SUB), 0)
    first_head = feat < HEAD_DIM
    n_blk = ATT_TILE // ATT_SUB
    units = [(blk, hh) for blk in range(n_blk) for hh in range(2)]
    kinds = _score_block_kinds()

    def scores(u):
        blk, hh = units[u]
        c0 = blk * ATT_SUB
        qt = qt_ref[0, :, c0:c0 + ATT_SUB]
        kwin = kcat[c0:c0 + ATT_WIN, :]
        variant = jnp.where(i == 0, blk + 1, 0) if blk + 1 < N_VARIANTS else 0
        mask = first_head if hh == 0 else jnp.logical_not(first_head)
        qm = jnp.where(mask, qt, jnp.zeros_like(qt))
        slot = u % ATT_SLOTS
        col_max = [None] * (ATT_SUB // LANES)
        for r0 in range(0, ATT_WIN, ATT_KEY_ROWS):
            raw = _dot(kwin[r0:r0 + ATT_KEY_ROWS], qm)
            for rb in range(r0 // CHUNK, (r0 + ATT_KEY_ROWS) // CHUNK):
                rows = slice(rb * CHUNK, (rb + 1) * CHUNK)
                for lt in range(ATT_SUB // LANES):
                    lanes = slice(lt * LANES, (lt + 1) * LANES)
                    kind = kinds[rb, lt] if isinstance(variant, int) else "bias"
                    if kind == "masked":
                        continue
                    st = raw[rb * CHUNK - r0:(rb + 1) * CHUNK - r0, lanes]
                    if kind == "bias":
                        st = st + tbl_ref[variant, hh, rows, lanes]
                    st_s[slot, rows, lanes] = st
                    part = jnp.max(st.reshape(CHUNK // SUBLANES, SUBLANES, LANES), axis=0)
                    col_max[lt] = part if col_max[lt] is None else jnp.maximum(col_max[lt], part)
        return [jnp.max(cm, axis=0, keepdims=True) for cm in col_max], isinstance(variant, int)

    def softmax_pv(u, stats):
        blk, hh = units[u]
        c0 = blk * ATT_SUB
        m, unpadded = stats
        slot = u % ATT_SLOTS
        for rb in range(ATT_WIN // CHUNK):
            rows = slice(rb * CHUNK, (rb + 1) * CHUNK)
            for lt in range(ATT_SUB // LANES):
                lanes = slice(lt * LANES, (lt + 1) * LANES)
                if unpadded and kinds[rb, lt] == "masked":
                    p_s[slot, rows, lanes] = jnp.zeros((CHUNK, LANES), BF16)
                else:
                    p_s[slot, rows, lanes] = jnp.exp2((st_s[slot, rows, lanes] - m[lt]).astype(BF16))
        vt = vtcat[hh * VT_ROWS:(hh + 1) * VT_ROWS, c0:c0 + ATT_WIN]
        ot = _dot(vt, p_s[u % ATT_SLOTS])
        return ot[:HEAD_DIM] * (1.0 / ot[HEAD_DIM:HEAD_DIM + 1])

    outs = []
    ms = [scores(u) for u in range(ATT_LOOKAHEAD)]
    for u in range(len(units)):
        if u + ATT_LOOKAHEAD < len(units):
            ms.append(scores(u + ATT_LOOKAHEAD))
        outs.append(softmax_pv(u, ms[u]))
        if u % 2 == 1:
            c0 = units[u][0] * ATT_SUB
            pair = jnp.concatenate(outs[-2:], axis=0)
            o_ref[0, 0, c0:c0 + ATT_SUB, :] = pair.T.astype(BF16)


def _attention(qt, k, vt, table):
    bsz, _, s, _ = k.shape
    t = ATT_TILE
    prev_idx = lambda i: jnp.maximum(i * (t // PAD) - 1, 0)
    return pl.pallas_call(
        _attn_kernel,
        grid=(HEAD_PAIRS, bsz, s // t),
        in_specs=[
            pl.BlockSpec((1, LANES, t), lambda p, b, i: (b, p, i)),
            pl.BlockSpec((1, 1, PAD, LANES), lambda p, b, i: (b, p, prev_idx(i), 0)),
            pl.BlockSpec((1, 1, t, LANES), lambda p, b, i: (b, p, i, 0)),
            pl.BlockSpec((1, LANES, PAD), lambda p, b, i: (b, p, prev_idx(i))),
            pl.BlockSpec((1, LANES, t), lambda p, b, i: (b, p, i)),
            pl.BlockSpec((N_VARIANTS, 2, ATT_WIN, ATT_SUB), lambda p, b, i: (0, p, 0, 0)),
        ],
        out_specs=pl.BlockSpec((1, 1, t, LANES), lambda p, b, i: (b, p, i, 0)),
        out_shape=jax.ShapeDtypeStruct(k.shape, BF16),
        scratch_shapes=[pltpu.VMEM((PAD + t, LANES), BF16), pltpu.VMEM((2 * VT_ROWS, PAD + t), BF16),
                        pltpu.VMEM((ATT_SLOTS, ATT_WIN, ATT_SUB), F32),
                        pltpu.VMEM((ATT_SLOTS, ATT_WIN, ATT_SUB), BF16)],
        compiler_params=pltpu.CompilerParams(
            dimension_semantics=("arbitrary", "arbitrary", "arbitrary"), vmem_limit_bytes=VMEM_LIMIT),
        name="band_attention",
    )(qt, k, k, vt, vt, table)


def kernel(x, a_norm, a_w_in, a_conv_w, a_conv_b, a_w_gate, a_b_gate, a_lambda, a_w_out, kv_norm, w_kv, k_norm, b_norm, b_w_q, b_q_norm, b_rel_bias, b_w_o, mlp_norm, w_up, w_down):
    bsz, s, d = x.shape
    assert d == D_MODEL and s % ATT_TILE == 0 and s % (L0_SUBS * REC_TILE) == 0 and s % KVQ_TILE == 0
    assert s % MLP_TILE == 0
    assert a_norm.shape[0] == 1 and b_norm.shape[0] == 1 and mlp_norm.shape[0] == 2
    n = bsz * s
    row = lambda p: p.reshape(1, -1).astype(F32)

    h = _layer0(x.reshape(n, d), s // (L0_SUBS * REC_TILE), row(a_norm[0]), a_w_in[0].astype(BF16), a_conv_w[0],
                row(a_conv_b[0]), a_w_gate[0].astype(BF16), a_b_gate[0], row(a_lambda[0]),
                a_w_out[0].astype(BF16), row(mlp_norm[0]), w_up[0].astype(BF16), w_down[0].astype(BF16))

    col = lambda g: jnp.broadcast_to(jnp.tile(g, N_HEADS).astype(F32)[:, None], (D_MODEL, LANES))
    q_gain = col(b_q_norm[0]) * (HEAD_DIM ** -0.5 * LOG2E)
    k_gain = col(k_norm)
    shp = (bsz, s, d)
    qt, k, vt = _kvq(h.reshape(shp), row(kv_norm), row(b_norm[0]), w_kv[:, :D_MODEL].T.astype(BF16),
                     w_kv[:, D_MODEL:].T.astype(BF16), b_w_q[0].T.astype(BF16), k_gain, q_gain)
    table = _bias_table(jnp.pad(b_rel_bias[0], ((0, 0), (0, 2 * LANES - NREL))))
    attn = _attention(qt, k, vt, table)
    out = _proj_mlp(attn, b_w_o[0].astype(BF16), h, row(mlp_norm[1]),
                    w_up[1].astype(BF16), w_down[1].astype(BF16))
    return out.reshape(shp)
```

```python
import functools

import jax
import jax.numpy as jnp
from jax import lax
from jax.experimental import pallas as pl
from jax.experimental.pallas import tpu as pltpu

D_MODEL = 1024
D_RNN = D_MODEL
LRU_BLOCKS = 8
LRU_BW = D_RNN // LRU_BLOCKS
CONV_W = 4
LRU_C = 8.0
N_HEADS = 16
HEAD_DIM = 64
CHUNK = 64
LEFT_CHUNKS = 8
PAD = LEFT_CHUNKS * CHUNK
MAX_REL = 2 * CHUNK
MIN_REL = -(CHUNK - 1)
NREL = MAX_REL - MIN_REL + 1
D_FF = 4 * D_MODEL
EPS = 1e-6

LANES = 128
SUBLANES = 8
TINY = 1.1754944e-38
NEG = -1e30

REC_TILE = 256
L0_SUBS = 2
L0_FF_CHUNK = 1024
L0_ORDER = ("rec", "conv", "g0", "m", "g1", "gate", "m", "s0", "s1", "m", "s2", "s3", "gelu", "m", "out")
MLP_TILE = 1024
KVQ_TILE = 1024
KVQ_SUB = 512
ATT_TILE = 8192
ATT_SUB = 256
ATT_WIN = PAD + ATT_SUB
ATT_LOOKAHEAD = 4
ATT_SLOTS = ATT_LOOKAHEAD + 2
ATT_KEY_ROWS = 384
VT_ROWS = HEAD_DIM + 16
LOG2E = 1.4426950408889634
HEAD_PAIRS = N_HEADS // 2
VMEM_LIMIT = 56 * 1024 * 1024

F32 = jnp.float32
BF16 = jnp.bfloat16


def _dot(a, b):
    return jnp.dot(a, b, preferred_element_type=F32)


def _rms_scale(x):
    return lax.rsqrt(jnp.mean(x * x, axis=-1, keepdims=True) + EPS)


def _const_spec(shape):
    zeros = (0,) * len(shape)
    return pl.BlockSpec(shape, lambda *_: zeros, pipeline_mode=pl.Buffered(1))


def _layer0_kernel(x_ref, norm_ref, w_in_ref, cw_ref, cb_ref, wg_ref, bg_ref, lam_ref, w_out_ref,
                   mnorm_ref, w_up_ref, w_down_ref, o_ref,
                   rec_ext, a_s, b_s, h_carry, mix_s, *, tiles_per_seq):
    t = REC_TILE
    j = pl.program_id(0)

    @pl.when(j % tiles_per_seq == 0)
    def _():
        rec_ext[:, 0:SUBLANES, :] = jnp.zeros((D_RNN // LANES, SUBLANES, LANES), F32)
        h_carry[...] = jnp.zeros_like(h_carry)

    @pl.when(j == 0)
    def _():
        mix_s[1] = jnp.zeros((L0_SUBS, t, D_MODEL), F32)

    slot = j % 2
    row = lax.broadcasted_iota(jnp.int32, (SUBLANES, D_RNN), 0)
    n_slabs = t // SUBLANES
    h_state = h_carry[0:1, :]
    for q in range(L0_SUBS):
        h_state = _layer0_sub_tile(
            q, slot, h_state, row, n_slabs, x_ref, norm_ref, w_in_ref, cw_ref, cb_ref, wg_ref, bg_ref,
            lam_ref, w_out_ref, mnorm_ref, w_up_ref, w_down_ref, o_ref, rec_ext, a_s, b_s, mix_s)
    h_carry[0:1, :] = h_state


def _layer0_sub_tile(q, slot, h_in, row, n_slabs, x_ref, norm_ref, w_in_ref, cw_ref, cb_ref, wg_ref,
                     bg_ref, lam_ref, w_out_ref, mnorm_ref, w_up_ref, w_down_ref, o_ref,
                     rec_ext, a_s, b_s, mix_s):
    t = REC_TILE
    rows_q = slice(q * t, (q + 1) * t)

    hp = mix_s[1 - slot, q]
    hn = (hp * _rms_scale(hp) * mnorm_ref[0:1, :]).astype(BF16)

    mlp_state = {"acc": hp, "k": 0}

    def mlp_chunk():
        k = mlp_state["k"]
        sl = slice(k * L0_FF_CHUNK, (k + 1) * L0_FF_CHUNK)
        up = jnp.maximum(_dot(hn, w_up_ref[:, sl]), 0.0)
        mlp_state["acc"] = mlp_state["acc"] + _dot((up * up).astype(BF16), w_down_ref[sl, :])
        mlp_state["k"] = k + 1

    x = x_ref[rows_q, :]
    xn = (x * _rms_scale(x) * norm_ref[0:1, :]).astype(BF16)
    v = {"h": h_in}

    def rec_proj():
        v["rec"] = _dot(xn, w_in_ref[:, D_RNN:])

    def gate_proj():
        v["gate"] = _dot(xn, w_in_ref[:, :D_RNN])

    def conv():
        cs = []
        for n in range(D_RNN // LANES):
            sl = slice(n * LANES, (n + 1) * LANES)
            rec = v["rec"][:, sl]
            rec_ext[n, SUBLANES:SUBLANES + t, :] = rec
            c = cb_ref[0:1, sl] + rec * cw_ref[CONV_W - 1:CONV_W, sl]
            for k in range(1, CONV_W):
                c = c + rec_ext[n, pl.ds(SUBLANES - k, t), :] * cw_ref[CONV_W - 1 - k:CONV_W - k, sl]
            cs.append(c)
            rec_ext[n, 0:SUBLANES, :] = rec_ext[n, t:t + SUBLANES, :]
        v["c"] = cs

    def gates(half):
        neg_lam = -lam_ref[0:1, :]
        softplus = jnp.maximum(neg_lam, 0.0) + jnp.log1p(jnp.exp(-jnp.abs(neg_lam)))
        for n in range(half * LRU_BLOCKS // 2, (half + 1) * LRU_BLOCKS // 2):
            sl = slice(n * LRU_BW, (n + 1) * LRU_BW)
            cb = v["c"][n]
            g = _dot(cb.astype(BF16), wg_ref[n]) + bg_ref[n:n + 1, :]
            r = jax.nn.sigmoid(g[:, :LRU_BW])
            i = jax.nn.sigmoid(g[:, LRU_BW:])
            log_a = (-LRU_C) * r * softplus[:, sl]
            a = jnp.exp(log_a)
            z = 1.0 - a * a
            mult = z * lax.rsqrt(jnp.maximum(z, TINY))
            a_s[q, :, sl] = a
            b_s[q, :, sl] = mult * (i * cb)

    def scan(quarter):
        for s in range(quarter * n_slabs // 4, (quarter + 1) * n_slabs // 4):
            rows = slice(s * SUBLANES, (s + 1) * SUBLANES)
            a = a_s[q, rows, :]
            b = b_s[q, rows, :]
            for d in (1, 2, 4):
                keep = row >= d
                a_sh = jnp.where(keep, pltpu.roll(a, d, 0), 1.0)
                b_sh = jnp.where(keep, pltpu.roll(b, d, 0), 0.0)
                b = a * b_sh + b
                a = a * a_sh
            h = a * v["h"] + b
            b_s[q, rows, :] = h
            v["h"] = h[SUBLANES - 1:SUBLANES, :]

    def gelu_gate():
        v["y"] = (jax.nn.gelu(v["gate"]) * b_s[q]).astype(BF16)

    def out_proj():
        mix_s[slot, q] = x + _dot(v["y"], w_out_ref[...])

    phases = {
        "rec": rec_proj, "gate": gate_proj, "conv": conv, "g0": lambda: gates(0), "g1": lambda: gates(1),
        "s0": lambda: scan(0), "s1": lambda: scan(1), "s2": lambda: scan(2), "s3": lambda: scan(3),
        "gelu": gelu_gate, "out": out_proj, "m": mlp_chunk,
    }
    for name in L0_ORDER:
        phases[name]()
    assert mlp_state["k"] == D_FF // L0_FF_CHUNK
    o_ref[rows_q, :] = mlp_state["acc"]
    return v["h"]


def _layer0(x2d, tiles_per_seq, norm, w_in, conv_w, conv_b, w_gate, b_gate, lam, w_out, mlp_norm, w_up, w_down):
    t = REC_TILE
    blk = L0_SUBS * t
    n_tiles = x2d.shape[0] // blk
    return pl.pallas_call(
        functools.partial(_layer0_kernel, tiles_per_seq=tiles_per_seq),
        grid=(n_tiles + 1,),
        in_specs=[
            pl.BlockSpec((blk, D_MODEL), lambda j: (jnp.minimum(j, n_tiles - 1), 0)),
            _const_spec((SUBLANES, D_MODEL)),
            _const_spec((D_MODEL, 2 * D_RNN)),
            _const_spec((CONV_W, D_RNN)),
            _const_spec((SUBLANES, D_RNN)),
            _const_spec((LRU_BLOCKS, LRU_BW, 2 * LRU_BW)),
            _const_spec((2 * LRU_BLOCKS, 2 * LRU_BW)),
            _const_spec((SUBLANES, D_RNN)),
            _const_spec((D_RNN, D_MODEL)),
            _const_spec((SUBLANES, D_MODEL)),
            _const_spec((D_MODEL, D_FF)),
            _const_spec((D_FF, D_MODEL)),
        ],
        out_specs=pl.BlockSpec((blk, D_MODEL), lambda j: (jnp.maximum(j - 1, 0), 0)),
        out_shape=jax.ShapeDtypeStruct(x2d.shape, F32),
        scratch_shapes=[
            pltpu.VMEM((D_RNN // LANES, t + SUBLANES, LANES), F32),
            pltpu.VMEM((L0_SUBS, t, D_RNN), F32),
            pltpu.VMEM((L0_SUBS, t, D_RNN), F32),
            pltpu.VMEM((SUBLANES, D_RNN), F32),
            pltpu.VMEM((2, L0_SUBS, t, D_MODEL), F32),
        ],
        compiler_params=pltpu.CompilerParams(
            dimension_semantics=("arbitrary",), vmem_limit_bytes=VMEM_LIMIT),
        name="layer0_mixer_mlp",
    )(x2d, norm, w_in, conv_w, conv_b, w_gate, b_gate, lam, w_out, mlp_norm, w_up, w_down)


FF_CHUNK = 1024


def _mlp_body(h, norm_ref, w_up_ref, w_down_ref):
    hn = (h * _rms_scale(h) * norm_ref[0:1, :]).astype(BF16)
    acc = h
    for j in range(D_FF // FF_CHUNK):
        sl = slice(j * FF_CHUNK, (j + 1) * FF_CHUNK)
        up = jnp.maximum(_dot(hn, w_up_ref[:, sl]), 0.0)
        acc = acc + _dot((up * up).astype(BF16), w_down_ref[sl, :])
    return acc


def _proj_mlp_kernel(a_ref, w_o_ref, x_ref, norm_ref, w_up_ref, w_down_ref, o_ref):
    attn = jnp.concatenate([a_ref[0, p] for p in range(HEAD_PAIRS)], axis=1)
    h = x_ref[...] + _dot(attn, w_o_ref[...])
    o_ref[...] = _mlp_body(h, norm_ref, w_up_ref, w_down_ref)


def _proj_mlp(attn, w_o, x2d, norm, w_up, w_down):
    n = x2d.shape[0]
    t = MLP_TILE
    tiles_per_seq = attn.shape[2] // t
    tile = pl.BlockSpec((t, D_MODEL), lambda i: (i, 0))
    attn_tile = pl.BlockSpec((1, HEAD_PAIRS, t, LANES), lambda i: (i // tiles_per_seq, 0, i % tiles_per_seq, 0))
    return pl.pallas_call(
        _proj_mlp_kernel,
        grid=(n // t,),
        in_specs=[attn_tile, _const_spec((D_MODEL, D_MODEL)), tile, _const_spec((SUBLANES, D_MODEL)),
                  _const_spec((D_MODEL, D_FF)), _const_spec((D_FF, D_MODEL))],
        out_specs=tile,
        out_shape=jax.ShapeDtypeStruct(x2d.shape, F32),
        compiler_params=pltpu.CompilerParams(
            dimension_semantics=("arbitrary",), vmem_limit_bytes=VMEM_LIMIT),
        name="proj_mlp",
    )(attn, w_o, x2d, norm, w_up, w_down)


def _dot_nt(a, b):
    return lax.dot_general(a, b, (((1,), (1,)), ((), ())), preferred_element_type=F32)


def _head_rms_normalise_t(zt, gain_ref):
    t = zt.shape[1]
    z3 = zt.reshape(N_HEADS, HEAD_DIM, t)
    inv = lax.rsqrt(jnp.mean(z3 * z3, axis=1, keepdims=True) + EPS)
    gain = jnp.concatenate([gain_ref[...]] * (t // LANES), axis=1)
    return (z3 * inv).reshape(N_HEADS * HEAD_DIM, t) * gain


def _kvq_kernel(x_ref, kvn_ref, qn_ref, w_kt_ref, w_vt_ref, w_qt_ref, kg_ref, qg_ref,
                qt_ref, k_ref, vt_ref):
    def project(r0):
        x = x_ref[0, r0:r0 + KVQ_SUB, :]
        xs = x * _rms_scale(x)
        xkv = (xs * kvn_ref[0:1, :]).astype(BF16)
        xq = (xs * qn_ref[0:1, :]).astype(BF16)
        return _dot_nt(w_vt_ref[...], xkv), _dot_nt(w_kt_ref[...], xkv), _dot_nt(w_qt_ref[...], xq)

    def finish(r0, vt, kt, qt):
        rows = slice(r0, r0 + KVQ_SUB)
        vt_ref[0, :, rows] = vt.astype(BF16)
        kn = _head_rms_normalise_t(kt, kg_ref).T.astype(BF16)
        for p in range(HEAD_PAIRS):
            k_ref[0, p, rows, :] = kn[:, p * LANES:(p + 1) * LANES]
        qt_ref[0, :, rows] = _head_rms_normalise_t(qt, qg_ref).astype(BF16)

    starts = list(range(0, KVQ_TILE, KVQ_SUB))
    pending = project(starts[0])
    for n, r0 in enumerate(starts):
        nxt = project(starts[n + 1]) if n + 1 < len(starts) else None
        finish(r0, *pending)
        pending = nxt


def _kvq(x, kv_norm, q_norm, w_kt, w_vt, w_qt, k_gain, q_gain):
    bsz, s, _ = x.shape
    t = KVQ_TILE
    tok_major = pl.BlockSpec((1, t, D_MODEL), lambda b, i: (b, i, 0))
    feat_major = pl.BlockSpec((1, D_MODEL, t), lambda b, i: (b, 0, i))
    w_spec = _const_spec((D_MODEL, D_MODEL))
    g_spec = _const_spec((D_MODEL, LANES))
    ft = jax.ShapeDtypeStruct((bsz, D_MODEL, s), BF16)
    return pl.pallas_call(
        _kvq_kernel,
        grid=(bsz, s // t),
        in_specs=[tok_major, _const_spec((SUBLANES, D_MODEL)), _const_spec((SUBLANES, D_MODEL)),
                  w_spec, w_spec, w_spec, g_spec, g_spec],
        out_specs=[feat_major, pl.BlockSpec((1, HEAD_PAIRS, t, LANES), lambda b, i: (b, 0, i, 0)), feat_major],
        out_shape=[ft, jax.ShapeDtypeStruct((bsz, HEAD_PAIRS, s, LANES), BF16), ft],
        compiler_params=pltpu.CompilerParams(
            dimension_semantics=("arbitrary", "arbitrary"), vmem_limit_bytes=VMEM_LIMIT),
        name="kvq_proj",
    )(x, kv_norm, q_norm, w_kt, w_vt, w_qt, k_gain, q_gain)


BASE_W = ATT_WIN + ATT_SUB
N_VARIANTS = 1 + PAD // ATT_SUB
ROLL_ROWS = 128
TABLE_HEADS = 4
BIAS_FREE_LO = ATT_SUB - CHUNK
BIAS_FREE_HI = PAD - MAX_REL


def _bias_kernel(rb_ref, o_ref, base_s):
    rb = (rb_ref[...] - rb_ref[:, NREL - 1:NREL]) * LOG2E
    r_idx = lax.broadcasted_iota(jnp.int32, (2 * LANES, BASE_W), 0)
    m_idx = lax.broadcasted_iota(jnp.int32, (2 * LANES, BASE_W), 1)
    want = jnp.clip(m_idx - ATT_SUB, MIN_REL, MAX_REL) - MIN_REL
    onehot = (r_idx == want).astype(BF16)
    p0 = rb.astype(BF16)
    r1 = rb - p0.astype(F32)
    p1 = r1.astype(BF16)
    p2 = (r1 - p1.astype(F32)).astype(BF16)
    base_s[...] = _dot(p0, onehot) + _dot(p1, onehot) + _dot(p2, onehot)
    for hl in range(TABLE_HEADS):
        base = base_s[pl.ds(pl.program_id(0) * TABLE_HEADS + hl, 1), :]
        for r0 in range(0, ATT_WIN, ROLL_ROWS):
            kj = r0 + lax.broadcasted_iota(jnp.int32, (ROLL_ROWS, ATT_SUB), 0)
            qi = lax.broadcasted_iota(jnp.int32, (ROLL_ROWS, ATT_SUB), 1)
            band_lo = (qi // CHUNK) * CHUNK
            in_band = (kj >= band_lo) & (kj < band_lo + PAD + CHUNK)
            rows = jnp.broadcast_to(base, (ROLL_ROWS, BASE_W))
            toeplitz = pltpu.roll(rows, r0, 1, stride=1, stride_axis=0)
            tile = jnp.where(in_band, toeplitz[:, ATT_WIN:], NEG)
            o_ref[0, hl, r0:r0 + ROLL_ROWS, :] = tile
            for v in range(1, N_VARIANTS):
                o_ref[v, hl, r0:r0 + ROLL_ROWS, :] = jnp.where(kj >= PAD - (v - 1) * ATT_SUB, tile, NEG)


def _bias_table(rel_bias_padded):
    return pl.pallas_call(
        _bias_kernel,
        grid=(N_HEADS // TABLE_HEADS,),
        in_specs=[_const_spec((N_HEADS, 2 * LANES))],
        out_specs=pl.BlockSpec((N_VARIANTS, TABLE_HEADS, ATT_WIN, ATT_SUB), lambda h: (0, h, 0, 0)),
        out_shape=jax.ShapeDtypeStruct((N_VARIANTS, N_HEADS, ATT_WIN, ATT_SUB), F32),
        scratch_shapes=[pltpu.VMEM((N_HEADS, BASE_W), F32)],
        compiler_params=pltpu.CompilerParams(
            dimension_semantics=("arbitrary",), vmem_limit_bytes=VMEM_LIMIT),
        name="rel_bias_table",
    )(rel_bias_padded)


def _attn_kernel(qt_ref, kp_ref, kc_ref, vtp_ref, vtc_ref, tbl_ref, o_ref, kcat, vtcat, st_s, p_s):
    i = pl.program_id(2)
    kcat[0:PAD, :] = kp_ref[0, 0]
    kcat[PAD:, :] = kc_ref[0, 0]
    for hh in range(2):
        src = slice(hh * HEAD_DIM, (hh + 1) * HEAD_DIM)
        r0 = hh * VT_ROWS
        vtcat[r0:r0 + HEAD_DIM, 0:PAD] = vtp_ref[0, src, :]
        vtcat[r0:r0 + HEAD_DIM, PAD:] = vtc_ref[0, src, :]
        vtcat[r0 + HEAD_DIM:r0 + VT_ROWS, :] = jnp.ones((VT_ROWS - HEAD_DIM, PAD + ATT_TILE), BF16)

    feat = lax.broadcasted_iota(jnp.int32, (LANES, ATT_SUB), 0)
    first_head = feat < HEAD_DIM
    n_blk = ATT_TILE // ATT_SUB
    units = [(blk, hh) for blk in range(n_blk) for hh in range(2)]

    def scores(u):
        blk, hh = units[u]
        c0 = blk * ATT_SUB
        qt = qt_ref[0, :, c0:c0 + ATT_SUB]
        kwin = kcat[c0:c0 + ATT_WIN, :]
        variant = jnp.where(i == 0, blk + 1, 0) if blk + 1 < N_VARIANTS else 0
        mask = first_head if hh == 0 else jnp.logical_not(first_head)
        qm = jnp.where(mask, qt, jnp.zeros_like(qt))
        m = None
        for r0 in range(0, ATT_WIN, ATT_KEY_ROWS):
            rows = slice(r0, r0 + ATT_KEY_ROWS)
            st = _dot(kwin[rows], qm)
            lo, hi = max(r0, BIAS_FREE_LO) - r0, min(r0 + ATT_KEY_ROWS, BIAS_FREE_HI) - r0
            if isinstance(variant, int) and lo < hi:
                parts = [st[:lo] + tbl_ref[variant, hh, r0:r0 + lo, :], st[lo:hi]]
                if hi < ATT_KEY_ROWS:
                    parts.append(st[hi:] + tbl_ref[variant, hh, r0 + hi:r0 + ATT_KEY_ROWS, :])
                st = jnp.concatenate(parts, axis=0)
            else:
                st = st + tbl_ref[variant, hh, rows, :]
            st_s[u % ATT_SLOTS, rows, :] = st
            piece_max = jnp.max(st, axis=0, keepdims=True)
            m = piece_max if m is None else jnp.maximum(m, piece_max)
        return m

    def softmax_pv(u, m):
        blk, hh = units[u]
        c0 = blk * ATT_SUB
        p_s[u % ATT_SLOTS] = jnp.exp2((st_s[u % ATT_SLOTS] - m).astype(BF16))
        vt = vtcat[hh * VT_ROWS:(hh + 1) * VT_ROWS, c0:c0 + ATT_WIN]
        ot = _dot(vt, p_s[u % ATT_SLOTS])
        return ot[:HEAD_DIM] * (1.0 / ot[HEAD_DIM:HEAD_DIM + 1])

    outs = []
    ms = [scores(u) for u in range(ATT_LOOKAHEAD)]
    for u in range(len(units)):
        if u + ATT_LOOKAHEAD < len(units):
            ms.append(scores(u + ATT_LOOKAHEAD))
        outs.append(softmax_pv(u, ms[u]))
        if u % 2 == 1:
            c0 = units[u][0] * ATT_SUB
            pair = jnp.concatenate(outs[-2:], axis=0)
            o_ref[0, 0, c0:c0 + ATT_SUB, :] = pair.T.astype(BF16)


def _attention(qt, k, vt, table):
    bsz, _, s, _ = k.shape
    t = ATT_TILE
    prev_idx = lambda i: jnp.maximum(i * (t // PAD) - 1, 0)
    return pl.pallas_call(
        _attn_kernel,
        grid=(HEAD_PAIRS, bsz, s // t),
        in_specs=[
            pl.BlockSpec((1, LANES, t), lambda p, b, i: (b, p, i)),
            pl.BlockSpec((1, 1, PAD, LANES), lambda p, b, i: (b, p, prev_idx(i), 0)),
            pl.BlockSpec((1, 1, t, LANES), lambda p, b, i: (b, p, i, 0)),
            pl.BlockSpec((1, LANES, PAD), lambda p, b, i: (b, p, prev_idx(i))),
            pl.BlockSpec((1, LANES, t), lambda p, b, i: (b, p, i)),
            pl.BlockSpec((N_VARIANTS, 2, ATT_WIN, ATT_SUB), lambda p, b, i: (0, p, 0, 0)),
        ],
        out_specs=pl.BlockSpec((1, 1, t, LANES), lambda p, b, i: (b, p, i, 0)),
        out_shape=jax.ShapeDtypeStruct(k.shape, BF16),
        scratch_shapes=[pltpu.VMEM((PAD + t, LANES), BF16), pltpu.VMEM((2 * VT_ROWS, PAD + t), BF16),
                        pltpu.VMEM((ATT_SLOTS, ATT_WIN, ATT_SUB), F32),
                        pltpu.VMEM((ATT_SLOTS, ATT_WIN, ATT_SUB), BF16)],
        compiler_params=pltpu.CompilerParams(
            dimension_semantics=("arbitrary", "arbitrary", "arbitrary"), vmem_limit_bytes=VMEM_LIMIT),
        name="band_attention",
    )(qt, k, k, vt, vt, table)


def kernel(x, a_norm, a_w_in, a_conv_w, a_conv_b, a_w_gate, a_b_gate, a_lambda, a_w_out, kv_norm, w_kv, k_norm, b_norm, b_w_q, b_q_norm, b_rel_bias, b_w_o, mlp_norm, w_up, w_down):
    bsz, s, d = x.shape
    assert d == D_MODEL and s % ATT_TILE == 0 and s % (L0_SUBS * REC_TILE) == 0 and s % KVQ_TILE == 0
    assert s % MLP_TILE == 0
    assert a_norm.shape[0] == 1 and b_norm.shape[0] == 1 and mlp_norm.shape[0] == 2
    n = bsz * s
    row = lambda p: jnp.broadcast_to(p.reshape(1, -1).astype(F32), (SUBLANES, p.size))

    h = _layer0(x.reshape(n, d), s // (L0_SUBS * REC_TILE), row(a_norm[0]), a_w_in[0].astype(BF16), a_conv_w[0],
                row(a_conv_b[0]), a_w_gate[0].astype(BF16), jnp.tile(a_b_gate[0], (2, 1)), row(a_lambda[0]),
                a_w_out[0].astype(BF16), row(mlp_norm[0]), w_up[0].astype(BF16), w_down[0].astype(BF16))

    col = lambda g: jnp.broadcast_to(jnp.tile(g, N_HEADS).astype(F32)[:, None], (D_MODEL, LANES))
    q_gain = col(b_q_norm[0]) * (HEAD_DIM ** -0.5 * LOG2E)
    k_gain = col(k_norm)
    shp = (bsz, s, d)
    qt, k, vt = _kvq(h.reshape(shp), row(kv_norm), row(b_norm[0]), w_kv[:, :D_MODEL].T.astype(BF16),
                     w_kv[:, D_MODEL:].T.astype(BF16), b_w_q[0].T.astype(BF16), k_gain, q_gain)
    table = _bias_table(jnp.pad(b_rel_bias[0], ((0, 0), (0, 2 * LANES - NREL))))
    attn = _attention(qt, k, vt, table)
    out = _proj_mlp(attn, b_w_o[0].astype(BF16), h, row(mlp_norm[1]),
                    w_up[1].astype(BF16), w_down[1].astype(BF16))
    return out.reshape(shp)
```
